```python
import math
import jax, jax.numpy as jnp
from jax import lax
import numpy as np

D_MODEL = 1024
BATCH = 16
SEQ = 4096
DEPTH = 1
DEC_BATCH = 128
DEC_SEQ = 8
PAST_LEN = 8192
PAGE_SIZE = 128

ATTN_GROUPS = ((128, 1), (512, 4), (2048, 16))
N_GROUPS = 3
ATTN_HEADS = 4
ATTN_HEAD_DIM = 64
ATTN_GROUP_WIDTH = ATTN_HEADS * ATTN_HEAD_DIM
ATTN_WIDTH = N_GROUPS * ATTN_GROUP_WIDTH
ATTN_BLOCK = 128

MLSTM_HEADS = 8
MLSTM_DK = 64
MLSTM_DV = 128
MLSTM_CHUNK = 64

N_EXPERTS = 256
TOP_K = 8
N_EXPERT_GROUPS = 8
TOPK_GROUPS = 4
D_EXPERT = 256
D_SHARED = 256
ROUTED_SCALE = 2.5
MOE_BLOCK = 128

ALPHA = (2 * DEPTH) ** 0.25
BETA = (8 * DEPTH) ** -0.25
EPS = 1e-5

IN_WIDTHS = (ATTN_WIDTH, ATTN_WIDTH, ATTN_WIDTH,
             MLSTM_HEADS * MLSTM_DK, MLSTM_HEADS * MLSTM_DK,
             MLSTM_HEADS * MLSTM_DV, MLSTM_HEADS * MLSTM_DV,
             MLSTM_HEADS, MLSTM_HEADS, D_MODEL, D_MODEL)
D_IN = sum(IN_WIDTHS)

kernel_name = 'hybrid_dilated_mlstm_moe_step'


def _layer_norm(x, g, b):
    xf = x.astype(jnp.float32)
    mu = jnp.mean(xf, axis=-1, keepdims=True)
    var = jnp.mean(jnp.square(xf - mu), axis=-1, keepdims=True)
    return ((xf - mu) * lax.rsqrt(var + EPS) * g + b).astype(x.dtype)


def _softmax_lse(s, mask):
    s = jnp.where(mask, s.astype(jnp.float32), -jnp.inf)
    mx = jnp.max(s, axis=-1, keepdims=True)
    p = jnp.exp(s - mx)
    den = jnp.sum(p, axis=-1, keepdims=True)
    return p / den, (mx + jnp.log(den))[..., 0]


def _dilated_band_attn(q, k, v, window, dilation):
    B, S, H, Dh = q.shape
    L = S // dilation
    W = window // dilation
    QB = ATTN_BLOCK
    nb = -(-L // QB)
    Lp = nb * QB
    N = B * dilation

    def to_lattice(t):
        t = t.reshape(B, L, dilation, H, Dh).transpose(0, 2, 1, 3, 4).reshape(N, L, H, Dh)
        return jnp.pad(t, ((0, 0), (0, Lp - L), (0, 0), (0, 0)))

    def band(t):
        t = jnp.pad(t, ((0, 0), (QB, 0), (0, 0), (0, 0))).reshape(N, nb + 1, QB, H, Dh)
        return jnp.concatenate([t[:, :-1], t[:, 1:]], axis=2)

    def from_lattice(t):
        t = t[:, :L].reshape((B, dilation, L) + t.shape[2:])
        return jnp.swapaxes(t, 1, 2).reshape((B, S) + t.shape[3:])

    qb = to_lattice(q).reshape(N, nb, QB, H, Dh)
    kb = band(to_lattice(k))
    vb = band(to_lattice(v))
    s = jnp.einsum('nbqhd,nbkhd->nbhqk', qb, kb) * (Dh ** -0.5)
    qi = jnp.arange(QB)[:, None]
    kj = jnp.arange(2 * QB)[None, :]
    dist = qi + QB - kj
    key_pos = jnp.arange(nb)[:, None, None] * QB + kj[None] - QB
    mask = (dist >= 0) & (dist <= W) & (key_pos >= 0)
    p, lse = _softmax_lse(s, mask[None, :, None])
    o = jnp.einsum('nbhqk,nbkhd->nbqhd', p.astype(v.dtype), vb).reshape(N, Lp, H, Dh)
    lse = lse.transpose(0, 1, 3, 2).reshape(N, Lp, H)
    return from_lattice(o), from_lattice(lse)


def _dilated_cached_attn(q, k, v, buf, window, dilation):
    B, T, H, Dh = q.shape
    Lw = buf.shape[1]
    W = window // dilation
    k_all = jnp.concatenate([buf[:, :, 0].astype(k.dtype), k], axis=1)
    v_all = jnp.concatenate([buf[:, :, 1].astype(v.dtype), v], axis=1)
    idx = Lw + jnp.arange(T)[:, None] - jnp.arange(W + 1)[None, :] * dilation
    kg = k_all[:, jnp.maximum(idx, 0)]
    vg = v_all[:, jnp.maximum(idx, 0)]
    s = jnp.einsum('bthd,btjhd->bthj', q, kg) * (Dh ** -0.5)
    p, lse = _softmax_lse(s, (idx >= 0)[None, :, None, :])
    o = jnp.einsum('bthj,btjhd->bthd', p.astype(v.dtype), vg)
    keep = min(window, Lw + T)
    new_buf = jnp.stack([k_all[:, Lw + T - keep:], v_all[:, Lw + T - keep:]], axis=2)
    return o, lse, new_buf


def _combine_dilations(outs, lses):
    wts = jax.nn.softmax(jnp.stack(lses), axis=0)
    o = jnp.einsum('gbsh,gbshd->bshd', wts, jnp.stack(outs).astype(jnp.float32))
    return o.astype(outs[0].dtype)


def _to_chunks(t, nc, L):
    B = t.shape[0]
    t = t.reshape((B, nc, L) + t.shape[2:])
    return t.transpose((1, 0, 3, 2) + tuple(range(4, t.ndim)))


def _mlstm(q, k, v, i_pre, log_f, C0, n0, m0):
    B, S, NH, DK = q.shape
    L = math.gcd(S, MLSTM_CHUNK)
    nc = S // L
    q = q * (DK ** -0.5)
    xs = (_to_chunks(q, nc, L), _to_chunks(k, nc, L), _to_chunks(v, nc, L),
          _to_chunks(i_pre, nc, L), _to_chunks(log_f, nc, L))
    causal = jnp.tril(jnp.ones((L, L), dtype=bool))

    def step(carry, chunk):
        C, n, m = carry
        qc, kc, vc, ic, fc = chunk
        b = jnp.cumsum(fc, axis=-1)
        dlog = jnp.where(causal, b[..., :, None] - b[..., None, :] + ic[..., None, :], -jnp.inf)
        m_inter = b + m[..., None]
        m_t = jnp.maximum(m_inter, jnp.max(dlog, axis=-1))
        sc = jnp.einsum('bhtk,bhsk->bhts', qc, kc) * jnp.exp(dlog - m_t[..., None])
        inter = jnp.exp(m_inter - m_t)
        num = jnp.einsum('bhts,bhsv->bhtv', sc, vc) + inter[..., None] * jnp.einsum('bhtk,bhkv->bhtv', qc, C)
        den = jnp.sum(sc, axis=-1) + inter * jnp.einsum('bhtk,bhk->bht', qc, n)
        h = num / jnp.maximum(jnp.abs(den), jnp.exp(-m_t))[..., None]
        bL = b[..., -1]
        g = bL[..., None] - b + ic
        m_new = jnp.maximum(bL + m, jnp.max(g, axis=-1))
        ws = jnp.exp(g - m_new[..., None])
        decay = jnp.exp(bL + m - m_new)
        C_new = decay[..., None, None] * C + jnp.einsum('bhs,bhsk,bhsv->bhkv', ws, kc, vc)
        n_new = decay[..., None] * n + jnp.einsum('bhs,bhsk->bhk', ws, kc)
        return (C_new, n_new, m_new), h

    (C1, n1, m1), h = lax.scan(step, (C0, n0, m0), xs)
    h = h.transpose(1, 0, 3, 2, 4).reshape(B, S, NH, -1)
    return h, C1, n1, m1


def _moe(x, w_router, router_bias, w_exp_in, w_exp_out, w_sh_in, w_sh_out):
    shape = x.shape
    xt = x.reshape(-1, shape[-1])
    N = xt.shape[0]
    scores = jax.nn.sigmoid((xt @ w_router).astype(jnp.float32))
    sel = scores + router_bias.astype(jnp.float32)
    per_group = N_EXPERTS // N_EXPERT_GROUPS
    group_score = jnp.sum(lax.top_k(sel.reshape(N, N_EXPERT_GROUPS, per_group), 2)[0], axis=-1)
    _, gidx = lax.top_k(group_score, TOPK_GROUPS)
    gmask = jnp.any(gidx[:, :, None] == jnp.arange(N_EXPERT_GROUPS)[None, None, :], axis=1)
    sel = jnp.where(jnp.repeat(gmask, per_group, axis=1), sel, -jnp.inf)
    _, top_idx = lax.top_k(sel, TOP_K)
    top_w = jnp.take_along_axis(scores, top_idx, axis=-1)
    top_w = top_w / jnp.sum(top_w, axis=-1, keepdims=True) * ROUTED_SCALE
    NK = N * TOP_K
    e_flat = top_idx.reshape(-1)
    order = jnp.argsort(e_flat)
    e_sorted = e_flat[order]
    counts = jnp.bincount(e_flat, length=N_EXPERTS)
    starts = jnp.cumsum(counts) - counts
    padded = (counts + MOE_BLOCK - 1) // MOE_BLOCK * MOE_BLOCK
    pends = jnp.cumsum(padded)
    dest = pends[e_sorted] - padded[e_sorted] + jnp.arange(NK) - starts[e_sorted]
    n_blocks = -(-(NK + N_EXPERTS * (MOE_BLOCK - 1)) // MOE_BLOCK)
    row_tok = jnp.full((n_blocks * MOE_BLOCK,), N, jnp.int32).at[dest].set((order // TOP_K).astype(jnp.int32))
    row_w = jnp.zeros((n_blocks * MOE_BLOCK,), jnp.float32).at[dest].set(top_w.reshape(-1)[order])
    blk_exp = jnp.minimum(jnp.searchsorted(pends, jnp.arange(n_blocks) * MOE_BLOCK, side='right'), N_EXPERTS - 1)
    x_pad = jnp.concatenate([xt, jnp.zeros((1, xt.shape[-1]), xt.dtype)], axis=0)

    def expert_block(acc, blk):
        tok, w, e = blk
        xb = x_pad[tok]
        a, g = jnp.split(xb @ w_exp_in[e], 2, axis=-1)
        yb = (jax.nn.silu(a) * g) @ w_exp_out[e]
        return acc.at[tok].add(yb * w[:, None].astype(yb.dtype)), None

    routed, _ = lax.scan(expert_block, jnp.zeros_like(x_pad),
                         (row_tok.reshape(n_blocks, MOE_BLOCK), row_w.reshape(n_blocks, MOE_BLOCK), blk_exp))
    a, g = jnp.split(xt @ w_sh_in, 2, axis=-1)
    shared = (jax.nn.silu(a) * g) @ w_sh_out
    return (routed[:N] + shared).reshape(shape)


def _layer(x, c, caches, params):
    (w_ada, b_ada, w_in, b_if, mlstm_norm_g, w_proj_attn, w_proj_mlstm, w_out,
     ln1_g, ln1_b, w_router, router_bias, w_exp_in, w_exp_out, w_sh_in, w_sh_out, ln2_g, ln2_b) = params
    B, S, _ = x.shape
    f32 = jnp.float32
    mod = jax.nn.silu(c) @ w_ada + b_ada
    shift1, scale1, gate1, shift2, scale2, gate2 = [m[:, None, :] for m in jnp.split(mod, 6, axis=-1)]

    h = x * (1 + scale1) + shift1
    z = h @ w_in
    offsets = [int(o) for o in np.cumsum(IN_WIDTHS)[:-1]]
    q_a, k_a, v_a, q_m, k_m, v_m, o_m, i_m, f_m, g_a, g_b = jnp.split(z, offsets, axis=-1)
    attn_shape = (B, S, N_GROUPS, ATTN_HEADS, ATTN_HEAD_DIM)
    q_a, k_a, v_a = q_a.reshape(attn_shape), k_a.reshape(attn_shape), v_a.reshape(attn_shape)
    outs, lses, bufs = [], [], []
    for gi, (window, dilation) in enumerate(ATTN_GROUPS):
        qg, kg, vg = q_a[:, :, gi], k_a[:, :, gi], v_a[:, :, gi]
        if caches is None:
            o, lse = _dilated_band_attn(qg, kg, vg, window, dilation)
            keep = min(window, S)
            buf = jnp.stack([kg[:, S - keep:], vg[:, S - keep:]], axis=2)
        else:
            o, lse, buf = _dilated_cached_attn(qg, kg, vg, caches[gi], window, dilation)
        outs.append(o)
        lses.append(lse)
        bufs.append(buf)
    attn_o = _combine_dilations(outs, lses).reshape(B, S, ATTN_GROUP_WIDTH)

    if caches is None:
        C0 = jnp.zeros((B, MLSTM_HEADS, MLSTM_DK, MLSTM_DV), f32)
        n0 = jnp.zeros((B, MLSTM_HEADS, MLSTM_DK), f32)
        m0 = jnp.zeros((B, MLSTM_HEADS), f32)
    else:
        C0, n0, m0 = caches[3].astype(f32), caches[4].astype(f32), caches[5].astype(f32)
    i_pre = i_m.astype(f32) + b_if[:MLSTM_HEADS].astype(f32)
    log_f = jax.nn.log_sigmoid(f_m.astype(f32) + b_if[MLSTM_HEADS:].astype(f32))
    hm, C1, n1, m1 = _mlstm(q_m.reshape(B, S, MLSTM_HEADS, MLSTM_DK).astype(f32),
                            k_m.reshape(B, S, MLSTM_HEADS, MLSTM_DK).astype(f32),
                            v_m.reshape(B, S, MLSTM_HEADS, MLSTM_DV).astype(f32),
                            i_pre, log_f, C0, n0, m0)
    mu = jnp.mean(hm, axis=-1, keepdims=True)
    var = jnp.mean(jnp.square(hm - mu), axis=-1, keepdims=True)
    hm = ((hm - mu) * lax.rsqrt(var + EPS)).reshape(B, S, MLSTM_HEADS * MLSTM_DV) * mlstm_norm_g
    mlstm_o = jax.nn.sigmoid(o_m) * hm.astype(x.dtype)

    merged = jax.nn.sigmoid(g_a) * (attn_o @ w_proj_attn) + jax.nn.sigmoid(g_b) * (mlstm_o @ w_proj_mlstm)
    y = merged @ w_out
    x = _layer_norm(ALPHA * x + gate1 * y, ln1_g, ln1_b)

    h2 = x * (1 + scale2) + shift2
    y2 = _moe(h2, w_router, router_bias, w_exp_in, w_exp_out, w_sh_in, w_sh_out)
    x = _layer_norm(ALPHA * x + gate2 * y2, ln2_g, ln2_b)
    return x, (bufs[0], bufs[1], bufs[2], C1, n1, m1)


def setup_inputs(seed: int = 0) -> dict:
    key = jax.random.key(seed)
    ks = iter(jax.random.split(key, 40))

    def nrm(shape, scale=1.0):
        return jax.random.normal(next(ks), shape, jnp.float32) * scale

    D = D_MODEL
    NH, DK, DV = MLSTM_HEADS, MLSTM_DK, MLSTM_DV
    x_prompt = nrm((BATCH, SEQ, D))
    x_sample = nrm((DEC_BATCH, DEC_SEQ, D))
    cache_kv_w128 = nrm((DEPTH, DEC_BATCH, min(ATTN_GROUPS[0][0], PAST_LEN), 2, ATTN_HEADS, ATTN_HEAD_DIM))
    cache_kv_w512 = nrm((DEPTH, DEC_BATCH, min(ATTN_GROUPS[1][0], PAST_LEN), 2, ATTN_HEADS, ATTN_HEAD_DIM))
    cache_kv_w2048 = nrm((DEPTH, DEC_BATCH, min(ATTN_GROUPS[2][0], PAST_LEN), 2, ATTN_HEADS, ATTN_HEAD_DIM))
    state_mlstm_C = nrm((DEPTH, DEC_BATCH, NH, DK, DV), 0.5)
    state_mlstm_n = nrm((DEPTH, DEC_BATCH, NH, DK), 0.5)
    state_mlstm_m = nrm((DEPTH, DEC_BATCH, NH), 1.0)
    c_prompt = nrm((BATCH, D))
    c_sample = nrm((DEC_BATCH, D))
    w_ada = nrm((DEPTH, D, 6 * D), 0.2 * D ** -0.5)
    gate_offset = jnp.zeros((6 * D,), jnp.float32).at[2 * D:3 * D].set(1.0).at[5 * D:].set(1.0)
    b_ada = gate_offset[None] + nrm((DEPTH, 6 * D), 0.02)
    w_in = nrm((DEPTH, D, D_IN), D ** -0.5)
    b_if = jnp.concatenate([nrm((DEPTH, NH), 0.1),
                            jnp.linspace(3.0, 6.0, NH, dtype=jnp.float32)[None] + nrm((DEPTH, NH), 0.1)], axis=-1)
    mlstm_norm_g = 1.0 + nrm((DEPTH, NH * DV), 0.1)
    w_proj_attn = nrm((DEPTH, ATTN_GROUP_WIDTH, D), BETA * ATTN_GROUP_WIDTH ** -0.5)
    w_proj_mlstm = nrm((DEPTH, NH * DV, D), BETA * (NH * DV) ** -0.5)
    w_out = nrm((DEPTH, D, D), BETA * D ** -0.5)
    ln1_g = 1.0 + nrm((DEPTH, D), 0.1)
    ln1_b = nrm((DEPTH, D), 0.02)
    w_router = nrm((DEPTH, D, N_EXPERTS), D ** -0.5)
    router_bias = nrm((DEPTH, N_EXPERTS), 0.01)
    w_exp_in = nrm((DEPTH, N_EXPERTS, D, 2 * D_EXPERT), D ** -0.5)
    w_exp_out = nrm((DEPTH, N_EXPERTS, D_EXPERT, D), BETA * D_EXPERT ** -0.5)
    w_sh_in = nrm((DEPTH, D, 2 * D_SHARED), D ** -0.5)
    w_sh_out = nrm((DEPTH, D_SHARED, D), BETA * D_SHARED ** -0.5)
    ln2_g = 1.0 + nrm((DEPTH, D), 0.1)
    ln2_b = nrm((DEPTH, D), 0.02)
    return {'x_prompt': x_prompt, 'x_sample': x_sample,
            'cache_kv_w128': cache_kv_w128, 'cache_kv_w512': cache_kv_w512, 'cache_kv_w2048': cache_kv_w2048,
            'state_mlstm_C': state_mlstm_C, 'state_mlstm_n': state_mlstm_n, 'state_mlstm_m': state_mlstm_m,
            'c_prompt': c_prompt, 'c_sample': c_sample,
            'w_ada': w_ada, 'b_ada': b_ada, 'w_in': w_in, 'b_if': b_if, 'mlstm_norm_g': mlstm_norm_g,
            'w_proj_attn': w_proj_attn, 'w_proj_mlstm': w_proj_mlstm, 'w_out': w_out,
            'ln1_g': ln1_g, 'ln1_b': ln1_b, 'w_router': w_router, 'router_bias': router_bias,
            'w_exp_in': w_exp_in, 'w_exp_out': w_exp_out, 'w_sh_in': w_sh_in, 'w_sh_out': w_sh_out,
            'ln2_g': ln2_g, 'ln2_b': ln2_b}


def reference(x_prompt, x_sample, cache_kv_w128, cache_kv_w512, cache_kv_w2048,
              state_mlstm_C, state_mlstm_n, state_mlstm_m, c_prompt, c_sample,
              w_ada, b_ada, w_in, b_if, mlstm_norm_g, w_proj_attn, w_proj_mlstm, w_out,
              ln1_g, ln1_b, w_router, router_bias, w_exp_in, w_exp_out, w_sh_in, w_sh_out,
              ln2_g, ln2_b):
    xp, xs = x_prompt, x_sample
    new_p, new_s = [], []
    for l in range(DEPTH):
        params = (w_ada[l], b_ada[l], w_in[l], b_if[l], mlstm_norm_g[l], w_proj_attn[l], w_proj_mlstm[l],
                  w_out[l], ln1_g[l], ln1_b[l], w_router[l], router_bias[l], w_exp_in[l], w_exp_out[l],
                  w_sh_in[l], w_sh_out[l], ln2_g[l], ln2_b[l])
        xp, st_p = _layer(xp, c_prompt, None, params)
        caches = (cache_kv_w128[l], cache_kv_w512[l], cache_kv_w2048[l],
                  state_mlstm_C[l], state_mlstm_n[l], state_mlstm_m[l])
        xs, st_s = _layer(xs, c_sample, caches, params)
        new_p.append(st_p)
        new_s.append(st_s)
    sp = [jnp.stack([st[i] for st in new_p]) for i in range(6)]
    ss = [jnp.stack([st[i] for st in new_s]) for i in range(6)]
    return (xp, xs, sp[0], sp[1], sp[2], sp[3], sp[4], sp[5], ss[0], ss[1], ss[2], ss[3], ss[4], ss[5])
```

```python
import functools

import numpy as np
import jax
import jax.numpy as jnp
from jax import lax
from jax.experimental import pallas as pl
from jax.experimental.pallas import tpu as pltpu

F32 = jnp.float32
BF16 = jnp.bfloat16
I32 = jnp.int32
U32 = jnp.uint32
HIGHEST = lax.Precision.HIGHEST
SDS = jax.ShapeDtypeStruct
NEG_INF = float("-inf")

ATTN_GROUPS = ((128, 1), (512, 4), (2048, 16))
N_GROUPS = 3
ATTN_HEADS = 4
ATTN_HEAD_DIM = 64
GROUP_WIDTH = ATTN_HEADS * ATTN_HEAD_DIM
ATTN_BLOCK = 128
MLSTM_HEADS = 8
MLSTM_DK = 64
MLSTM_DV = 128
N_EXPERTS = 256
TOP_K = 8
N_EXPERT_GROUPS = 8
TOPK_GROUPS = 4
D_EXPERT = 256
D_SHARED = 256
ROUTED_SCALE = 2.5
DEPTH = 1
ALPHA = (2 * DEPTH) ** 0.25
EPS = 1e-5

LANES = 128
MLSTM_CHUNK = 128
DK_PAD = LANES
MOE_TILE = 128
MOE_BLOCK = 256
ROUTE_TILE = 512
VMEM_LIMIT = 56 << 20


def _cp(*sem, vmem=VMEM_LIMIT):
    return pltpu.CompilerParams(dimension_semantics=sem, vmem_limit_bytes=vmem)


def _nt_dot(a, b):
    return lax.dot_general(a, b, (((1,), (1,)), ((), ())), preferred_element_type=F32)


def _layer_norm(r, g, b):
    mu = jnp.mean(r, axis=-1, keepdims=True)
    var = jnp.mean(jnp.square(r - mu), axis=-1, keepdims=True)
    return (r - mu) * lax.rsqrt(var + EPS) * g + b


def _mod_spec(mod3, tm, tiles_per_batch):
    d = mod3.shape[-1]
    if mod3.shape[1] == 1:
        return pl.BlockSpec((1, 1, d), lambda i: (i // tiles_per_batch, 0, 0))
    return pl.BlockSpec((1, tm, d), lambda i: (i, 0, 0))


def _ada_kernel(c_ref, w_ref, b_ref, o_ref):
    c = c_ref[...]
    s = c * jax.nn.sigmoid(c)
    o_ref[...] = jnp.dot(s, w_ref[...], precision=HIGHEST, preferred_element_type=F32) + b_ref[...]


def _ada(c_all, w_ada, b_ada):
    r, d = c_all.shape
    n = w_ada.shape[1]
    tn = 1024
    return pl.pallas_call(
        _ada_kernel,
        grid=(n // tn,),
        in_specs=[pl.BlockSpec((r, d), lambda j: (0, 0)),
                  pl.BlockSpec((d, tn), lambda j: (0, j)),
                  pl.BlockSpec((1, tn), lambda j: (0, j))],
        out_specs=pl.BlockSpec((r, tn), lambda j: (0, j)),
        out_shape=SDS((r, n), F32),
        compiler_params=_cp("parallel"),
        name="ada",
    )(c_all, w_ada, b_ada.reshape(1, n))


def _proj_kernel(n_w, has_hp, x_ref, sc_ref, sh_ref, *refs):
    w_refs = refs[:n_w]
    pos = n_w
    if has_hp:
        whp_ref = refs[pos]
        pos += 1
    o_refs = refs[pos:pos + n_w]
    h = x_ref[...] * (1.0 + sc_ref[0]) + sh_ref[0]
    hb = h.astype(BF16)
    for w_ref, o_ref in zip(w_refs, o_refs):
        o_ref[...] = jnp.dot(hb, w_ref[...], preferred_element_type=F32).astype(o_ref.dtype)
    if has_hp:
        ohp_ref = refs[pos + n_w]
        ohp_ref[...] = jnp.dot(h, whp_ref[...], precision=HIGHEST, preferred_element_type=F32)


def _proj(x, sc3, sh3, weights, out_dtypes, w_hp, tm, tiles_per_batch, name):
    n, d = x.shape
    n_w = len(weights)
    has_hp = w_hp is not None
    in_specs = [pl.BlockSpec((tm, d), lambda i: (i, 0)),
                _mod_spec(sc3, tm, tiles_per_batch), _mod_spec(sh3, tm, tiles_per_batch)]
    in_specs += [pl.BlockSpec(w.shape, lambda i: (0, 0)) for w in weights]
    out_specs = [pl.BlockSpec((tm, w.shape[1]), lambda i: (i, 0)) for w in weights]
    out_shape = [SDS((n, w.shape[1]), dt) for w, dt in zip(weights, out_dtypes)]
    args = [x, sc3, sh3, *weights]
    if has_hp:
        in_specs.append(pl.BlockSpec(w_hp.shape, lambda i: (0, 0)))
        out_specs.append(pl.BlockSpec((tm, w_hp.shape[1]), lambda i: (i, 0)))
        out_shape.append(SDS((n, w_hp.shape[1]), F32))
        args.append(w_hp)
    return pl.pallas_call(
        functools.partial(_proj_kernel, n_w, has_hp),
        grid=(n // tm,),
        in_specs=in_specs, out_specs=out_specs, out_shape=out_shape,
        compiler_params=_cp("parallel"),
        name=name,
    )(*args)


def _head_masks(width):
    lane_head = lax.broadcasted_iota(I32, (1, width), 1) // ATTN_HEAD_DIM
    return [(lane_head == h).astype(F32) for h in range(ATTN_HEADS)]


def _band_attn_kernel(wband, q_ref, kc_ref, kp_ref, vc_ref, vp_ref, o_ref, l_ref):
    j = pl.program_id(2)
    qb = q_ref.shape[0]
    q = q_ref[...].astype(F32)
    k = jnp.concatenate([kp_ref[...], kc_ref[...]], axis=0).astype(BF16)
    v = jnp.concatenate([vp_ref[...], vc_ref[...]], axis=0).astype(BF16)
    qi = lax.broadcasted_iota(I32, (qb, 2 * qb), 0)
    kj = lax.broadcasted_iota(I32, (qb, 2 * qb), 1)
    dist = qi + qb - kj
    valid = jnp.where(dist >= 0, 1, 0) * jnp.where(dist <= wband, 1, 0) * jnp.where(j * qb + kj - qb >= 0, 1, 0)
    mask = valid > 0
    o_acc = jnp.zeros(o_ref.shape, F32)
    l_acc = jnp.zeros(l_ref.shape, F32)
    for hm in _head_masks(q.shape[1]):
        qh = (q * hm).astype(BF16)
        s = _nt_dot(qh, k) * (ATTN_HEAD_DIM ** -0.5)
        s = jnp.where(mask, s, NEG_INF)
        mx = jnp.max(s, axis=-1, keepdims=True)
        p = jnp.exp(s - mx)
        den = jnp.sum(p, axis=-1, keepdims=True)
        oh = jnp.dot((p / den).astype(BF16), v, preferred_element_type=F32)
        o_acc = o_acc + oh * hm
        l_acc = l_acc + (mx + jnp.log(den)) * hm
    o_ref[...] = o_acc
    l_ref[...] = l_acc


def _band_attn(q_a, kv_a, batch, seq, g, window, dilation):
    lat = seq // dilation
    nb = lat // ATTN_BLOCK
    gw = GROUP_WIDTH
    qv = q_a.reshape(batch, lat, dilation * N_GROUPS * gw)
    kvv = kv_a.reshape(batch, lat, dilation * 2 * N_GROUPS * gw)
    blk = (None, ATTN_BLOCK, gw)
    cur = lambda off, per: (lambda b, r, j: (b, j, r * per + off))
    prev = lambda off, per: (lambda b, r, j: (b, jnp.maximum(j - 1, 0), r * per + off))
    o, l = pl.pallas_call(
        functools.partial(_band_attn_kernel, window // dilation),
        grid=(batch, dilation, nb),
        in_specs=[pl.BlockSpec(blk, cur(g, N_GROUPS)),
                  pl.BlockSpec(blk, cur(g, 2 * N_GROUPS)),
                  pl.BlockSpec(blk, prev(g, 2 * N_GROUPS)),
                  pl.BlockSpec(blk, cur(N_GROUPS + g, 2 * N_GROUPS)),
                  pl.BlockSpec(blk, prev(N_GROUPS + g, 2 * N_GROUPS))],
        out_specs=[pl.BlockSpec(blk, lambda b, r, j: (b, j, r)),
                   pl.BlockSpec(blk, lambda b, r, j: (b, j, r))],
        out_shape=[SDS((batch, lat, dilation * gw), F32)] * 2,
        compiler_params=_cp("parallel", "parallel", "arbitrary"),
        name=f"band_attn_d{dilation}",
    )(qv, kvv, kvv, kvv, kvv)
    return o.reshape(batch * seq, gw), l.reshape(batch * seq, gw)


def _cache_attn_kernel(wband, dilation, c_ref, q_ref, k_ref, v_ref, o_ref, l_ref, nc_ref):
    lw = c_ref.shape[0]
    t_new = q_ref.shape[0]
    gw = q_ref.shape[1]
    q = q_ref[...]
    kn = k_ref[...]
    vn = v_ref[...]
    nc_ref[0:lw - t_new, :] = c_ref[t_new:lw, :]
    nc_ref[lw - t_new:lw, 0:gw] = kn
    nc_ref[lw - t_new:lw, gw:2 * gw] = vn
    kc = c_ref[:, 0:gw].astype(BF16)
    vc = c_ref[:, gw:2 * gw].astype(BF16)
    knb = kn.astype(BF16)
    vnb = vn.astype(BF16)

    def band_mask(n_keys, key_base):
        t = lax.broadcasted_iota(I32, (t_new, n_keys), 0)
        p = lax.broadcasted_iota(I32, (t_new, n_keys), 1) + key_base
        delta = lw + t - p
        ok = (jnp.where(delta >= 0, 1, 0) * jnp.where((delta & (dilation - 1)) == 0, 1, 0)
              * jnp.where(delta <= wband * dilation, 1, 0))
        return ok > 0

    mask_c = band_mask(lw, 0)
    mask_n = band_mask(t_new, lw)
    o_acc = jnp.zeros(o_ref.shape, F32)
    l_acc = jnp.zeros(l_ref.shape, F32)
    for hm in _head_masks(gw):
        qh = (q * hm).astype(BF16)
        sc = jnp.where(mask_c, _nt_dot(qh, kc) * (ATTN_HEAD_DIM ** -0.5), NEG_INF)
        sn = jnp.where(mask_n, _nt_dot(qh, knb) * (ATTN_HEAD_DIM ** -0.5), NEG_INF)
        mx = jnp.maximum(jnp.max(sc, axis=-1, keepdims=True), jnp.max(sn, axis=-1, keepdims=True))
        pc = jnp.exp(sc - mx)
        pn = jnp.exp(sn - mx)
        den = jnp.sum(pc, axis=-1, keepdims=True) + jnp.sum(pn, axis=-1, keepdims=True)
        oh = (jnp.dot((pc / den).astype(BF16), vc, preferred_element_type=F32)
              + jnp.dot((pn / den).astype(BF16), vnb, preferred_element_type=F32))
        o_acc = o_acc + oh * hm
        l_acc = l_acc + (mx + jnp.log(den)) * hm
    o_ref[...] = o_acc
    l_ref[...] = l_acc


def _cache_attn(q_s, kv_s, cache, g, window, dilation):
    b, lw = cache.shape[0], cache.shape[1]
    t_new = q_s.shape[1]
    gw = GROUP_WIDTH
    assert lw == window and lw % dilation == 0 and t_new % 8 == 0 and dilation & (dilation - 1) == 0
    cv = cache.reshape(b, lw, 2 * gw)
    blk = (None, t_new, gw)
    o, l, nc = pl.pallas_call(
        functools.partial(_cache_attn_kernel, window // dilation, dilation),
        grid=(b,),
        in_specs=[pl.BlockSpec((None, lw, 2 * gw), lambda i: (i, 0, 0)),
                  pl.BlockSpec(blk, lambda i: (i, 0, g)),
                  pl.BlockSpec(blk, lambda i: (i, 0, g)),
                  pl.BlockSpec(blk, lambda i: (i, 0, N_GROUPS + g))],
        out_specs=[pl.BlockSpec(blk, lambda i: (i, 0, 0)),
                   pl.BlockSpec(blk, lambda i: (i, 0, 0)),
                   pl.BlockSpec((None, lw, 2 * gw), lambda i: (i, 0, 0))],
        out_shape=[SDS((b, t_new, gw), F32), SDS((b, t_new, gw), F32), SDS((b, lw, 2 * gw), F32)],
        compiler_params=_cp("parallel"),
        name=f"cache_attn_d{dilation}",
    )(cv, q_s, kv_s, kv_s)
    return o.reshape(b * t_new, gw), l.reshape(b * t_new, gw), nc


def _log_sigmoid(x):
    return jnp.minimum(x, 0.0) - jnp.log1p(jnp.exp(-jnp.abs(x)))


def _mlstm_kernel(n_valid, q_ref, k_ref, v_ref, o_ref, g_ref, bif_ref, ng_ref, c0_ref, n0_ref, m0_ref,
                  h_ref, c1_ref, n1_ref, m1_ref, c_sc, n_sc, m_sc):
    ci = pl.program_id(1)
    nh, dk, dv = MLSTM_HEADS, MLSTM_DK, MLSTM_DV
    L = MLSTM_CHUNK

    @pl.when(ci == 0)
    def _():
        c_sc[...] = jnp.zeros(c_sc.shape, F32)
        n_sc[...] = jnp.zeros(n_sc.shape, F32)
        m_sc[...] = jnp.zeros(m_sc.shape, F32)
        c_sc[:, 0:dk, :] = c0_ref[...]
        n_sc[:, 0:dk] = n0_ref[...]
        m_sc[0:1, 0:nh] = m0_ref[...]

    def rows(ref):
        x = ref[...].astype(F32)
        if n_valid < L:
            x = jnp.concatenate([x, jnp.zeros((L - n_valid, x.shape[1]), F32)], axis=0)
        return x

    q = rows(q_ref).astype(BF16)
    k = rows(k_ref)
    v = rows(v_ref).astype(BF16)
    og = rows(o_ref)
    graw = rows(g_ref) + bif_ref[...]
    lane = lax.broadcasted_iota(I32, (L, LANES), 1)
    row = lax.broadcasted_iota(I32, (L, LANES), 0)
    is_i = lane < nh
    is_f = (lane >= nh) & (lane < 2 * nh)
    live = row < n_valid
    gi = jnp.where(is_i, jnp.where(live, graw, NEG_INF), 0.0)
    gf = jnp.where(is_f & live, _log_sigmoid(graw), 0.0)
    tril = (lax.broadcasted_iota(I32, (L, L), 0) >= lax.broadcasted_iota(I32, (L, L), 1))
    bcol = jnp.dot(tril.astype(F32), gf, precision=HIGHEST, preferred_element_type=F32)
    brow = bcol.T
    irow = gi.T
    scale = dk ** -0.5
    for h in range(nh):
        sl = slice(h * LANES, (h + 1) * LANES)
        qh = q[:, sl]
        kh = k[:, sl]
        vh = v[:, sl]
        b_c = bcol[:, nh + h:nh + h + 1]
        b_r = brow[nh + h:nh + h + 1, :]
        i_r = irow[h:h + 1, :]
        m_prev = m_sc[0:1, h:h + 1]
        dlog = jnp.where(tril, b_c - b_r + i_r, NEG_INF)
        m_inter = b_c + m_prev
        m_t = jnp.maximum(m_inter, jnp.max(dlog, axis=-1, keepdims=True))
        s = _nt_dot(qh, kh.astype(BF16)) * scale
        sc = s * jnp.exp(dlog - m_t)
        inter = jnp.exp(m_inter - m_t)
        c_h = c_sc[h]
        n_h = n_sc[h:h + 1, :]
        num = (jnp.dot(sc.astype(BF16), vh, preferred_element_type=F32)
               + inter * (jnp.dot(qh, c_h.astype(BF16), preferred_element_type=F32) * scale))
        qn = jnp.sum(qh.astype(F32) * n_h, axis=-1, keepdims=True) * scale
        den = jnp.sum(sc, axis=-1, keepdims=True) + inter * qn
        hh = num / jnp.maximum(jnp.abs(den), jnp.exp(-m_t))
        b_last = bcol[L - 1:L, nh + h:nh + h + 1]
        g_r = b_last - b_r + i_r
        m_new = jnp.maximum(b_last + m_prev, jnp.max(g_r, axis=-1, keepdims=True))
        ws = jnp.exp(g_r - m_new)
        decay = jnp.exp(b_last + m_prev - m_new)
        kw = (kh.T * ws).astype(BF16)
        c_sc[h] = decay * c_h + jnp.dot(kw, vh, preferred_element_type=F32)
        ws8 = jnp.broadcast_to(ws, (8, L)).astype(BF16)
        n_sc[h:h + 1, :] = decay * n_h + jnp.dot(ws8, kh.astype(BF16), preferred_element_type=F32)[0:1, :]
        m_sc[0:1, h:h + 1] = m_new
        mu = jnp.mean(hh, axis=-1, keepdims=True)
        var = jnp.mean(jnp.square(hh - mu), axis=-1, keepdims=True)
        hn = (hh - mu) * lax.rsqrt(var + EPS) * ng_ref[:, sl]
        out = jax.nn.sigmoid(og[:, sl]) * hn
        h_ref[:, sl] = out[0:n_valid, :].astype(h_ref.dtype)

    @pl.when(ci == pl.num_programs(1) - 1)
    def _():
        c1_ref[...] = c_sc[:, 0:dk, :]
        n1_ref[...] = n_sc[:, 0:dk]
        m1_ref[...] = m_sc[0:1, 0:nh]


def _mlstm(qm, km, vm, om, gates, bif_pad, norm_g, c0, n0, m0, n_valid, out_dtype, name):
    b, s, w = qm.shape
    nh, dk, dv = MLSTM_HEADS, MLSTM_DK, MLSTM_DV
    nc = s // n_valid
    assert s % n_valid == 0 and (n_valid == MLSTM_CHUNK or nc == 1)
    tok = lambda width: pl.BlockSpec((None, n_valid, width), lambda i, c: (i, c, 0))
    const = lambda shape: pl.BlockSpec(shape, lambda i, c: (0,) * len(shape))
    h, c1, n1, m1 = pl.pallas_call(
        functools.partial(_mlstm_kernel, n_valid),
        grid=(b, nc),
        in_specs=[tok(w), tok(w), tok(w), tok(w), tok(LANES), const((1, LANES)), const((1, w)),
                  pl.BlockSpec((None, nh, dk, dv), lambda i, c: (i, 0, 0, 0)),
                  pl.BlockSpec((None, nh, dk), lambda i, c: (i, 0, 0)),
                  pl.BlockSpec((None, 1, nh), lambda i, c: (i, 0, 0))],
        out_specs=[tok(w),
                   pl.BlockSpec((None, nh, dk, dv), lambda i, c: (i, 0, 0, 0)),
                   pl.BlockSpec((None, nh, dk), lambda i, c: (i, 0, 0)),
                   pl.BlockSpec((None, 1, nh), lambda i, c: (i, 0, 0))],
        out_shape=[SDS((b, s, w), out_dtype), SDS((b, nh, dk, dv), F32), SDS((b, nh, dk), F32),
                   SDS((b, 1, nh), F32)],
        scratch_shapes=[pltpu.VMEM((nh, DK_PAD, dv), F32), pltpu.VMEM((nh, DK_PAD), F32),
                        pltpu.VMEM((8, LANES), F32)],
        compiler_params=_cp("parallel", "arbitrary"),
        name=name,
    )(qm, km, vm, om, gates, bif_pad, norm_g, c0, n0, m0.reshape(b, 1, nh))
    return h, c1, n1, m1.reshape(b, nh)


def _pack_bf16_pairs(x):
    w = x.shape[1] // 2
    bits = lax.bitcast_convert_type(x.astype(BF16).astype(F32), U32)
    return (bits[:, :w] >> 16) | (bits[:, w:] & jnp.uint32(0xFFFF0000))


def _unpack_bf16_pairs(p):
    lo = lax.bitcast_convert_type(p << 16, F32).astype(BF16)
    hi = lax.bitcast_convert_type(p & jnp.uint32(0xFFFF0000), F32).astype(BF16)
    return lo, hi


def _merge_kernel(x_ref, sc1_ref, sh1_ref, g1_ref, sc2_ref, sh2_ref,
                  o0_ref, o1_ref, o2_ref, l0_ref, l1_ref, l2_ref, mo_ref,
                  wg_ref, wpa_ref, wpm_ref, wout_ref, lng_ref, lnb_ref, x1_ref, h2p_ref):
    x = x_ref[...]
    d = x.shape[1]
    h = (x * (1.0 + sc1_ref[0]) + sh1_ref[0]).astype(BF16)
    g = jnp.dot(h, wg_ref[...], preferred_element_type=F32)
    l0, l1, l2 = l0_ref[...], l1_ref[...], l2_ref[...]
    lm = jnp.maximum(jnp.maximum(l0, l1), l2)
    e0, e1, e2 = jnp.exp(l0 - lm), jnp.exp(l1 - lm), jnp.exp(l2 - lm)
    ao = (e0 * o0_ref[...] + e1 * o1_ref[...] + e2 * o2_ref[...]) / (e0 + e1 + e2)
    pa = jnp.dot(ao.astype(BF16), wpa_ref[...], preferred_element_type=F32)
    pm = jnp.dot(mo_ref[...].astype(BF16), wpm_ref[...], preferred_element_type=F32)
    merged = jax.nn.sigmoid(g[:, :d]) * pa + jax.nn.sigmoid(g[:, d:]) * pm
    y = jnp.dot(merged.astype(BF16), wout_ref[...], preferred_element_type=F32)
    x1 = _layer_norm(ALPHA * x + g1_ref[0] * y, lng_ref[...], lnb_ref[...])
    x1_ref[...] = x1
    h2p_ref[...] = _pack_bf16_pairs(x1 * (1.0 + sc2_ref[0]) + sh2_ref[0])


def _merge(x, mods, attn, mo, wg, wpa, wpm, wout, lng, lnb, tm, tiles_per_batch, name):
    n, d = x.shape
    tok = lambda width: pl.BlockSpec((tm, width), lambda i: (i, 0))
    const = lambda a: pl.BlockSpec(a.shape, lambda i: (0,) * a.ndim)
    return pl.pallas_call(
        _merge_kernel,
        grid=(n // tm,),
        in_specs=[tok(d)] + [_mod_spec(m, tm, tiles_per_batch) for m in mods]
        + [tok(GROUP_WIDTH)] * 6 + [tok(mo.shape[1])]
        + [const(a) for a in (wg, wpa, wpm, wout, lng, lnb)],
        out_specs=[tok(d), tok(d // 2)],
        out_shape=[SDS((n, d), F32), SDS((n, d // 2), U32)],
        compiler_params=_cp("parallel"),
        name=name,
    )(x, *mods, *attn, mo, wg, wpa, wpm, wout, lng, lnb)


def _first_index_of_max(x, idx, big):
    mx = jnp.max(x, axis=0, keepdims=True)
    first = jnp.min(jnp.where(x == mx, idx, big), axis=0, keepdims=True)
    return mx, first


def _route_kernel(hp_ref, wlo_ref, whi_ref, bias_ref, idx_ref, w_ref, rank_ref, cnt_ref):
    tm = hp_ref.shape[0]
    ne = N_EXPERTS
    per = ne // N_EXPERT_GROUPS
    lo, hi = _unpack_bf16_pairs(hp_ref[...])
    logits = _nt_dot(wlo_ref[...], lo) + _nt_dot(whi_ref[...], hi)
    scores = jax.nn.sigmoid(logits)
    sel = scores + bias_ref[...]
    erow = lax.broadcasted_iota(I32, (ne, tm), 0).astype(F32)
    prow = lax.broadcasted_iota(I32, (per, tm), 0).astype(F32)
    gs = []
    for g in range(N_EXPERT_GROUPS):
        xg = sel[g * per:(g + 1) * per, :]
        m1, i1 = _first_index_of_max(xg, prow, per)
        m2 = jnp.max(jnp.where(prow == i1, NEG_INF, xg), axis=0, keepdims=True)
        gs.append(m1 + m2)
    gs = jnp.concatenate(gs, axis=0)
    grow = lax.broadcasted_iota(I32, gs.shape, 0).astype(F32)
    gkeep = jnp.zeros(gs.shape, F32)
    for _ in range(TOPK_GROUPS):
        _, gi = _first_index_of_max(gs, grow, N_EXPERT_GROUPS)
        pick = grow == gi
        gkeep = jnp.where(pick, 1.0, gkeep)
        gs = jnp.where(pick, NEG_INF, gs)
    keep = jnp.concatenate([jnp.broadcast_to(gkeep[g:g + 1, :], (per, tm)) for g in range(N_EXPERT_GROUPS)], axis=0)
    cand = jnp.where(keep > 0, sel, NEG_INF)
    member = jnp.zeros((ne, tm), F32)
    picks, idxs, ws = [], [], []
    for _ in range(TOP_K):
        _, ei = _first_index_of_max(cand, erow, ne)
        pick = erow == ei
        picks.append(pick)
        idxs.append(ei)
        ws.append(jnp.sum(jnp.where(pick, scores, 0.0), axis=0, keepdims=True))
        cand = jnp.where(pick, NEG_INF, cand)
        member = jnp.where(pick, 1.0, member)
    wsum = ws[0]
    for wk in ws[1:]:
        wsum = wsum + wk
    idx_ref[...] = jnp.concatenate(idxs, axis=0).astype(I32)
    w_ref[...] = jnp.concatenate(ws, axis=0) / wsum * ROUTED_SCALE
    ti = lax.broadcasted_iota(I32, (tm, tm), 0)
    tj = lax.broadcasted_iota(I32, (tm, tm), 1)
    before = (ti < tj).astype(BF16)
    mb = member.astype(BF16)
    prefix = jnp.dot(mb, before, preferred_element_type=F32)
    rank_ref[...] = jnp.concatenate(
        [jnp.sum(jnp.where(p, prefix, 0.0), axis=0, keepdims=True) for p in picks], axis=0).astype(I32)
    cnt_ref[...] = _nt_dot(jnp.ones((8, tm), BF16), mb).astype(I32)


def _route(h2p, w_router_t, bias_col):
    n, half = h2p.shape
    tm = ROUTE_TILE
    nt = n // tm
    wlo = w_router_t[:, :half]
    whi = w_router_t[:, half:]
    pair = lambda dt: SDS((TOP_K, n), dt)
    return pl.pallas_call(
        _route_kernel,
        grid=(nt,),
        in_specs=[pl.BlockSpec((tm, half), lambda i: (i, 0)),
                  pl.BlockSpec(wlo.shape, lambda i: (0, 0)),
                  pl.BlockSpec(whi.shape, lambda i: (0, 0)),
                  pl.BlockSpec(bias_col.shape, lambda i: (0, 0))],
        out_specs=[pl.BlockSpec((TOP_K, tm), lambda i: (0, i))] * 3
        + [pl.BlockSpec((None, 8, N_EXPERTS), lambda i: (i, 0, 0))],
        out_shape=[pair(I32), pair(F32), pair(I32), SDS((nt, 8, N_EXPERTS), I32)],
        compiler_params=_cp("parallel"),
        name="route",
    )(h2p, wlo, whi, bias_col)


def _dest_kernel(idx_ref, rank_ref, base_ref, dest_ref):
    tm = idx_ref.shape[1]
    erow = lax.broadcasted_iota(I32, (N_EXPERTS, tm), 0)
    base = base_ref[...]
    idx = idx_ref[...]
    rows = [jnp.sum(jnp.where(erow == idx[k:k + 1, :], base, 0.0), axis=0, keepdims=True) for k in range(TOP_K)]
    dest_ref[...] = jnp.concatenate(rows, axis=0).astype(I32) + rank_ref[...]


def _dest(idx, rank, base_cols, tile):
    n = idx.shape[1]
    per_route = ROUTE_TILE // tile
    return pl.pallas_call(
        _dest_kernel,
        grid=(n // tile,),
        in_specs=[pl.BlockSpec((TOP_K, tile), lambda i: (0, i)),
                  pl.BlockSpec((TOP_K, tile), lambda i: (0, i)),
                  pl.BlockSpec((None, N_EXPERTS, 1), lambda i: (i // per_route, 0, 0))],
        out_specs=pl.BlockSpec((None, TOP_K, tile), lambda i: (i, 0, 0)),
        out_shape=SDS((n // tile, TOP_K, tile), I32),
        compiler_params=_cp("parallel"),
        name="dest",
    )(idx, rank, base_cols)


def _row_copy_loop(tile, make_copy, start):
    def body(t, carry):
        for k in range(TOP_K):
            cp = make_copy(k, t)
            if start:
                cp.start()
            else:
                cp.wait()
        return carry
    lax.fori_loop(0, tile, body, 0, unroll=8)


def _scatter_kernel(dest_hbm, hp_ref, xs_in, xs_out, dest_smem, sem_idx, sem_rows):
    del xs_in
    i = pl.program_id(0)
    tile = hp_ref.shape[0]
    n_idx = TOP_K * tile
    idx_cp = pltpu.make_async_copy(dest_hbm.at[pl.ds(pl.multiple_of(i * n_idx, n_idx), n_idx)], dest_smem, sem_idx)
    idx_cp.start()
    idx_cp.wait()

    def row_copy(k, t):
        return pltpu.make_async_copy(hp_ref.at[pl.ds(t, 1)], xs_out.at[pl.ds(dest_smem[k * tile + t], 1)], sem_rows)

    _row_copy_loop(tile, row_copy, True)
    _row_copy_loop(tile, row_copy, False)


def _scatter_rows(dest_flat, h2p, n_rows):
    n, half = h2p.shape
    tile = MOE_TILE
    xs0 = jnp.zeros((n_rows, half), U32)
    return pl.pallas_call(
        _scatter_kernel,
        grid=(n // tile,),
        in_specs=[pl.BlockSpec(memory_space=pl.ANY),
                  pl.BlockSpec((tile, half), lambda i: (i, 0)),
                  pl.BlockSpec(memory_space=pl.ANY)],
        out_specs=pl.BlockSpec(memory_space=pl.ANY),
        out_shape=SDS((n_rows, half), U32),
        scratch_shapes=[pltpu.SMEM((TOP_K * tile,), I32), pltpu.SemaphoreType.DMA, pltpu.SemaphoreType.DMA],
        input_output_aliases={2: 0},
        compiler_params=_cp("arbitrary"),
        name="scatter_rows",
    )(dest_flat, h2p, xs0)


def _expert_kernel(blk_exp_ref, n_used_ref, xs_ref, win_ref, wout_ref, ys_ref, win_sc, wout_sc):
    i = pl.program_id(0)
    prev = blk_exp_ref[jnp.maximum(i - 1, 0)]

    @pl.when((i == 0) | (blk_exp_ref[i] != prev))
    def _():
        win_sc[...] = win_ref[...].astype(BF16)
        wout_sc[...] = wout_ref[...].astype(BF16)

    @pl.when(i < n_used_ref[0])
    def _():
        half = xs_ref.shape[1]
        lo, hi = _unpack_bf16_pairs(xs_ref[...])
        ag = (jnp.dot(lo, win_sc[0:half, :], preferred_element_type=F32)
              + jnp.dot(hi, win_sc[half:2 * half, :], preferred_element_type=F32))
        a = ag[:, :D_EXPERT]
        g = ag[:, D_EXPERT:]
        mid = (a * jax.nn.sigmoid(a) * g).astype(BF16)
        ys_ref[...] = jnp.dot(mid, wout_sc[...], preferred_element_type=F32)

    @pl.when(i >= n_used_ref[0])
    def _():
        ys_ref[...] = jnp.zeros(ys_ref.shape, F32)


def _experts(xs, blk_exp, n_used, w_exp_in, w_exp_out):
    n_rows, half = xs.shape
    d = 2 * half
    nb = n_rows // MOE_BLOCK
    grid_spec = pltpu.PrefetchScalarGridSpec(
        num_scalar_prefetch=2,
        grid=(nb,),
        in_specs=[pl.BlockSpec((MOE_BLOCK, half), lambda i, be, nu: (i, 0)),
                  pl.BlockSpec((None, d, 2 * D_EXPERT), lambda i, be, nu: (be[i], 0, 0)),
                  pl.BlockSpec((None, D_EXPERT, d), lambda i, be, nu: (be[i], 0, 0))],
        out_specs=pl.BlockSpec((MOE_BLOCK, d), lambda i, be, nu: (i, 0)),
        scratch_shapes=[pltpu.VMEM((d, 2 * D_EXPERT), BF16), pltpu.VMEM((D_EXPERT, d), BF16)],
    )
    return pl.pallas_call(
        _expert_kernel,
        grid_spec=grid_spec,
        out_shape=SDS((n_rows, d), F32),
        compiler_params=_cp("arbitrary"),
        name="experts",
    )(blk_exp, n_used, xs, w_exp_in, w_exp_out)


def _final_kernel(dest_hbm, x1_ref, hp_ref, w_ref, g2_ref, ys_hbm, wsin_ref, wsout_ref, lng_ref, lnb_ref,
                  y_ref, rows_sc, dest_smem, sem_idx, sem_rows, *, tile_offset):
    i = pl.program_id(0)
    tile = x1_ref.shape[0]
    n_idx = TOP_K * tile
    idx_cp = pltpu.make_async_copy(
        dest_hbm.at[pl.ds(pl.multiple_of((i + tile_offset) * n_idx, n_idx), n_idx)], dest_smem, sem_idx)
    idx_cp.start()
    idx_cp.wait()

    def row_copy(k, t):
        return pltpu.make_async_copy(ys_hbm.at[pl.ds(dest_smem[k * tile + t], 1)], rows_sc.at[k, pl.ds(t, 1)], sem_rows)

    _row_copy_loop(tile, row_copy, True)
    half = hp_ref.shape[1]
    lo, hi = _unpack_bf16_pairs(hp_ref[...])
    ag = (jnp.dot(lo, wsin_ref[0:half, :], preferred_element_type=F32)
          + jnp.dot(hi, wsin_ref[half:2 * half, :], preferred_element_type=F32))
    a = ag[:, :D_SHARED]
    g = ag[:, D_SHARED:]
    y2 = jnp.dot((a * jax.nn.sigmoid(a) * g).astype(BF16), wsout_ref[...], preferred_element_type=F32)
    w_cols = jnp.concatenate([w_ref[...], jnp.zeros((tile - TOP_K, tile), F32)], axis=0).T
    _row_copy_loop(tile, row_copy, False)
    for k in range(TOP_K):
        y2 = y2 + w_cols[:, k:k + 1] * rows_sc[k]
    y_ref[...] = _layer_norm(ALPHA * x1_ref[...] + g2_ref[0] * y2, lng_ref[...], lnb_ref[...])


def _final(dest_flat, x1, h2p, w_top, g2_3, ys, wsin, wsout, lng, lnb, tile_offset, tiles_per_batch, name):
    n, d = x1.shape
    tile = MOE_TILE
    assert tile == LANES
    const = lambda a: pl.BlockSpec(a.shape, lambda i: (0,) * a.ndim)
    return pl.pallas_call(
        functools.partial(_final_kernel, tile_offset=tile_offset),
        grid=(n // tile,),
        in_specs=[pl.BlockSpec(memory_space=pl.ANY),
                  pl.BlockSpec((tile, d), lambda i: (i, 0)),
                  pl.BlockSpec((tile, d // 2), lambda i: (i, 0)),
                  pl.BlockSpec((TOP_K, tile), lambda i: (0, i)),
                  _mod_spec(g2_3, tile, tiles_per_batch),
                  pl.BlockSpec(memory_space=pl.ANY),
                  const(wsin), const(wsout), const(lng), const(lnb)],
        out_specs=pl.BlockSpec((tile, d), lambda i: (i, 0)),
        out_shape=SDS((n, d), F32),
        scratch_shapes=[pltpu.VMEM((TOP_K, tile, d), F32), pltpu.SMEM((TOP_K * tile,), I32),
                        pltpu.SemaphoreType.DMA, pltpu.SemaphoreType.DMA],
        compiler_params=_cp("arbitrary"),
        name=name,
    )(dest_flat, x1, h2p, w_top, g2_3, ys, wsin, wsout, lng, lnb)


def _sorted_layout(counts):
    totals = jnp.sum(counts, axis=0)
    padded = (totals + MOE_BLOCK - 1) // MOE_BLOCK * MOE_BLOCK
    pends = jnp.cumsum(padded)
    base = (pends - padded)[None, :] + jnp.cumsum(counts, axis=0) - counts
    return base.astype(I32), pends


def _moe_routed(h2p, w_router, router_bias, w_exp_in, w_exp_out):
    n = h2p.shape[0]
    idx, w_top, rank, cnt = _route(h2p, w_router.T.astype(BF16), router_bias.reshape(N_EXPERTS, 1))
    base, pends = _sorted_layout(cnt[:, 0, :])
    n_blocks = -(-(n * TOP_K + N_EXPERTS * (MOE_BLOCK - 1)) // MOE_BLOCK)
    blk_exp = jnp.minimum(jnp.searchsorted(pends, jnp.arange(n_blocks, dtype=I32) * MOE_BLOCK, side="right"),
                          N_EXPERTS - 1).astype(I32)
    n_used = (pends[-1:] // MOE_BLOCK).astype(I32)
    dest = _dest(idx, rank, base.astype(F32)[:, :, None], MOE_TILE).reshape(-1)
    xs = _scatter_rows(dest, h2p, n_blocks * MOE_BLOCK)
    ys = _experts(xs, blk_exp, n_used, w_exp_in, w_exp_out)
    return dest, w_top, ys


def _pad_heads(w, nh, dk):
    d = w.shape[0]
    return jnp.pad(w.reshape(d, nh, dk), ((0, 0), (0, 0), (0, DK_PAD - dk))).reshape(d, nh * DK_PAD)


def _split_w_in(w_in):
    aw = N_GROUPS * GROUP_WIDTH
    nh, dk, dv = MLSTM_HEADS, MLSTM_DK, MLSTM_DV
    offs = np.cumsum([0, aw, aw, aw, nh * dk, nh * dk, nh * dv, nh * dv, nh, nh, w_in.shape[0], w_in.shape[0]])
    seg = [w_in[:, offs[i]:offs[i + 1]] for i in range(11)]
    q_a, k_a, v_a, q_m, k_m, v_m, o_m, i_m, f_m, g_a, g_b = seg
    bf = lambda a: a.astype(BF16)
    w_gates = jnp.pad(jnp.concatenate([i_m, f_m], axis=1), ((0, 0), (0, LANES - 2 * nh)))
    return dict(qa=bf(q_a), kva=bf(jnp.concatenate([k_a, v_a], axis=1)),
                qm=bf(_pad_heads(q_m, nh, dk)), km=bf(_pad_heads(k_m, nh, dk)), vm=bf(v_m), om=bf(o_m),
                gates=w_gates, g=bf(jnp.concatenate([g_a, g_b], axis=1)))


def _mixing(x, mods, tm, tiles_per_batch, batch, seq, caches, states, wts, prompt):
    n, d = x.shape
    sc1, sh1, g1 = mods["scale1"], mods["shift1"], mods["gate1"]
    sc2, sh2 = mods["scale2"], mods["shift2"]
    act = BF16 if prompt else F32
    q_a, kv_a = _proj(x, sc1, sh1, [wts["qa"], wts["kva"]], [act, F32], None, tm, tiles_per_batch,
                      "proj_attn_p" if prompt else "proj_attn_s")
    qm, km, vm, om, gates = _proj(x, sc1, sh1, [wts["qm"], wts["km"], wts["vm"], wts["om"]],
                                  [act, act, act, F32], wts["gates"], tm, tiles_per_batch,
                                  "proj_mlstm_p" if prompt else "proj_mlstm_s")
    attn_o, attn_l, bufs = [], [], []
    for g, (window, dilation) in enumerate(ATTN_GROUPS):
        if prompt:
            o, l = _band_attn(q_a, kv_a, batch, seq, g, window, dilation)
            keep = min(window, seq)
            kv3 = kv_a.reshape(batch, seq, 2, N_GROUPS, GROUP_WIDTH)[:, seq - keep:, :, g, :]
            bufs.append(kv3.reshape(1, batch, keep, 2, ATTN_HEADS, ATTN_HEAD_DIM))
        else:
            o, l, nc = _cache_attn(q_a.reshape(batch, seq, -1), kv_a.reshape(batch, seq, -1), caches[g], g,
                                   window, dilation)
            bufs.append(nc.reshape(1, batch, nc.shape[1], 2, ATTN_HEADS, ATTN_HEAD_DIM))
        attn_o.append(o)
        attn_l.append(l)
    r3 = lambda a: a.reshape(batch, seq, a.shape[-1])
    n_valid = MLSTM_CHUNK if prompt else seq
    mo, c1, n1, m1 = _mlstm(r3(qm), r3(km), r3(vm), r3(om), r3(gates), wts["bif"], wts["norm_g"],
                            states[0], states[1], states[2], n_valid, act,
                            "mlstm_p" if prompt else "mlstm_s")
    x1, h2p = _merge(x, [sc1, sh1, g1, sc2, sh2], attn_o + attn_l, mo.reshape(n, -1),
                     wts["g"], wts["pa"], wts["pm"], wts["out"], wts["ln1_g"], wts["ln1_b"],
                     tm, tiles_per_batch, "merge_p" if prompt else "merge_s")
    return x1, h2p, bufs, (c1[None], n1[None], m1[None])


def _mod_pieces(mod, d, rows_per_batch, tm):
    names = ("shift1", "scale1", "gate1", "shift2", "scale2", "gate2")
    out = {}
    for p, name in enumerate(names):
        piece = mod[:, p * d:(p + 1) * d]
        if rows_per_batch % tm == 0:
            out[name] = piece[:, None, :]
        else:
            out[name] = jnp.repeat(piece, rows_per_batch, axis=0).reshape(-1, tm, d)
    return out


def kernel(x_prompt, x_sample, cache_kv_w128, cache_kv_w512, cache_kv_w2048, state_mlstm_C, state_mlstm_n, state_mlstm_m, c_prompt, c_sample, w_ada, b_ada, w_in, b_if, mlstm_norm_g, w_proj_attn, w_proj_mlstm, w_out, ln1_g, ln1_b, w_router, router_bias, w_exp_in, w_exp_out, w_sh_in, w_sh_out, ln2_g, ln2_b):
    assert w_ada.shape[0] == DEPTH
    bp, sp, d = x_prompt.shape
    bs, ss, _ = x_sample.shape
    nh = MLSTM_HEADS
    np_, ns = bp * sp, bs * ss

    mod = _ada(jnp.concatenate([c_prompt, c_sample], axis=0), w_ada[0], b_ada[0])
    wts = _split_w_in(w_in[0])
    wts.update(
        bif=jnp.pad(b_if[0], (0, LANES - 2 * nh)).reshape(1, LANES),
        norm_g=mlstm_norm_g[0].reshape(1, -1),
        pa=w_proj_attn[0].astype(BF16), pm=w_proj_mlstm[0].astype(BF16), out=w_out[0].astype(BF16),
        ln1_g=ln1_g[0].reshape(1, d), ln1_b=ln1_b[0].reshape(1, d))

    tm_p, tm_s = 512, 256
    mods_p = _mod_pieces(mod[:bp], d, sp, tm_p)
    mods_s = _mod_pieces(mod[bp:], d, ss, tm_s)
    zeros_p = (jnp.zeros((bp, nh, MLSTM_DK, MLSTM_DV), F32), jnp.zeros((bp, nh, MLSTM_DK), F32),
               jnp.zeros((bp, nh), F32))
    x1p, h2p_p, bufs_p, st_p = _mixing(x_prompt.reshape(np_, d), mods_p, tm_p, sp // tm_p, bp, sp, None,
                                       zeros_p, wts, True)
    caches = (cache_kv_w128[0], cache_kv_w512[0], cache_kv_w2048[0])
    states = (state_mlstm_C[0], state_mlstm_n[0], state_mlstm_m[0])
    x1s, h2p_s, bufs_s, st_s = _mixing(x_sample.reshape(ns, d), mods_s, tm_s, 1, bs, ss, caches, states, wts, False)

    h2p = jnp.concatenate([h2p_p, h2p_s], axis=0)
    dest, w_top, ys = _moe_routed(h2p, w_router[0], router_bias[0], w_exp_in[0], w_exp_out[0])
    wsin, wsout = w_sh_in[0].astype(BF16), w_sh_out[0].astype(BF16)
    lng, lnb = ln2_g[0].reshape(1, d), ln2_b[0].reshape(1, d)
    g2_p = _mod_pieces(mod[:bp], d, sp, MOE_TILE)["gate2"]
    g2_s = _mod_pieces(mod[bp:], d, ss, MOE_TILE)["gate2"]
    y_p = _final(dest, x1p, h2p_p, w_top[:, :np_], g2_p, ys, wsin, wsout, lng, lnb, 0, sp // MOE_TILE, "final_p")
    y_s = _final(dest, x1s, h2p_s, w_top[:, np_:], g2_s, ys, wsin, wsout, lng, lnb, np_ // MOE_TILE, 1, "final_s")

    return (y_p.reshape(bp, sp, d), y_s.reshape(bs, ss, d),
            bufs_p[0], bufs_p[1], bufs_p[2], st_p[0], st_p[1], st_p[2],
            bufs_s[0], bufs_s[1], bufs_s[2], st_s[0], st_s[1], st_s[2])
```

```python
import functools

import numpy as np
import jax
import jax.numpy as jnp
from jax import lax
from jax.experimental import pallas as pl
from jax.experimental.pallas import tpu as pltpu

F32 = jnp.float32
BF16 = jnp.bfloat16
I32 = jnp.int32
U32 = jnp.uint32
HIGHEST = lax.Precision.HIGHEST
SDS = jax.ShapeDtypeStruct
NEG_INF = float("-inf")

ATTN_GROUPS = ((128, 1), (512, 4), (2048, 16))
N_GROUPS = 3
ATTN_HEADS = 4
ATTN_HEAD_DIM = 64
GROUP_WIDTH = ATTN_HEADS * ATTN_HEAD_DIM
ATTN_BLOCK = 128
MLSTM_HEADS = 8
MLSTM_DK = 64
MLSTM_DV = 128
N_EXPERTS = 256
TOP_K = 8
N_EXPERT_GROUPS = 8
TOPK_GROUPS = 4
D_EXPERT = 256
D_SHARED = 256
ROUTED_SCALE = 2.5
DEPTH = 1
ALPHA = (2 * DEPTH) ** 0.25
EPS = 1e-5

LANES = 128
MLSTM_CHUNK = 128
DK_PAD = LANES
MOE_TILE = 128
MOE_BLOCK = 512
ROUTE_TILE = 512
VMEM_LIMIT = 56 << 20


def _cp(*sem, vmem=VMEM_LIMIT):
    return pltpu.CompilerParams(dimension_semantics=sem, vmem_limit_bytes=vmem)


def _nt_dot(a, b):
    return lax.dot_general(a, b, (((1,), (1,)), ((), ())), preferred_element_type=F32)


def _layer_norm(r, g, b):
    mu = jnp.mean(r, axis=-1, keepdims=True)
    var = jnp.mean(jnp.square(r - mu), axis=-1, keepdims=True)
    return (r - mu) * lax.rsqrt(var + EPS) * g + b


def _mod_spec(mod3, tm, tiles_per_batch):
    d = mod3.shape[-1]
    if mod3.shape[1] == 1:
        return pl.BlockSpec((1, 1, d), lambda i: (i // tiles_per_batch, 0, 0))
    return pl.BlockSpec((1, tm, d), lambda i: (i, 0, 0))


def _ada_kernel(c_ref, w_ref, b_ref, o_ref):
    c = c_ref[...]
    s = c * jax.nn.sigmoid(c)
    o_ref[...] = jnp.dot(s, w_ref[...], precision=HIGHEST, preferred_element_type=F32) + b_ref[...]


def _ada(c_all, w_ada, b_ada):
    r, d = c_all.shape
    n = w_ada.shape[1]
    tn = 1024
    return pl.pallas_call(
        _ada_kernel,
        grid=(n // tn,),
        in_specs=[pl.BlockSpec((r, d), lambda j: (0, 0)),
                  pl.BlockSpec((d, tn), lambda j: (0, j)),
                  pl.BlockSpec((1, tn), lambda j: (0, j))],
        out_specs=pl.BlockSpec((r, tn), lambda j: (0, j)),
        out_shape=SDS((r, n), F32),
        compiler_params=_cp("parallel"),
        name="ada",
    )(c_all, w_ada, b_ada.reshape(1, n))


def _proj_kernel(n_w, has_hp, x_ref, sc_ref, sh_ref, *refs):
    w_refs = refs[:n_w]
    pos = n_w
    if has_hp:
        whp_ref = refs[pos]
        pos += 1
    o_refs = refs[pos:pos + n_w]
    h = x_ref[...] * (1.0 + sc_ref[0]) + sh_ref[0]
    hb = h.astype(BF16)
    for w_ref, o_ref in zip(w_refs, o_refs):
        o_ref[...] = jnp.dot(hb, w_ref[...], preferred_element_type=F32).astype(o_ref.dtype)
    if has_hp:
        ohp_ref = refs[pos + n_w]
        ohp_ref[...] = jnp.dot(h, whp_ref[...], precision=HIGHEST, preferred_element_type=F32)


def _proj(x, sc3, sh3, weights, out_dtypes, w_hp, tm, tiles_per_batch, name):
    n, d = x.shape
    n_w = len(weights)
    has_hp = w_hp is not None
    in_specs = [pl.BlockSpec((tm, d), lambda i: (i, 0)),
                _mod_spec(sc3, tm, tiles_per_batch), _mod_spec(sh3, tm, tiles_per_batch)]
    in_specs += [pl.BlockSpec(w.shape, lambda i: (0, 0)) for w in weights]
    out_specs = [pl.BlockSpec((tm, w.shape[1]), lambda i: (i, 0)) for w in weights]
    out_shape = [SDS((n, w.shape[1]), dt) for w, dt in zip(weights, out_dtypes)]
    args = [x, sc3, sh3, *weights]
    if has_hp:
        in_specs.append(pl.BlockSpec(w_hp.shape, lambda i: (0, 0)))
        out_specs.append(pl.BlockSpec((tm, w_hp.shape[1]), lambda i: (i, 0)))
        out_shape.append(SDS((n, w_hp.shape[1]), F32))
        args.append(w_hp)
    return pl.pallas_call(
        functools.partial(_proj_kernel, n_w, has_hp),
        grid=(n // tm,),
        in_specs=in_specs, out_specs=out_specs, out_shape=out_shape,
        compiler_params=_cp("parallel"),
        name=name,
    )(*args)


def _head_masks(width):
    lane_head = lax.broadcasted_iota(I32, (1, width), 1) // ATTN_HEAD_DIM
    return [(lane_head == h).astype(F32) for h in range(ATTN_HEADS)]


def _band_attn_kernel(wband, q_ref, kc_ref, kp_ref, vc_ref, vp_ref, o_ref, l_ref):
    j = pl.program_id(2)
    qb = q_ref.shape[0]
    q = q_ref[...].astype(F32)
    k = jnp.concatenate([kp_ref[...], kc_ref[...]], axis=0).astype(BF16)
    v = jnp.concatenate([vp_ref[...], vc_ref[...]], axis=0).astype(BF16)
    qi = lax.broadcasted_iota(I32, (qb, 2 * qb), 0)
    kj = lax.broadcasted_iota(I32, (qb, 2 * qb), 1)
    dist = qi + qb - kj
    valid = jnp.where(dist >= 0, 1, 0) * jnp.where(dist <= wband, 1, 0) * jnp.where(j * qb + kj - qb >= 0, 1, 0)
    bias = jnp.where(valid > 0, 0.0, NEG_INF)
    hms = _head_masks(q.shape[1])
    qs = jnp.concatenate([q * (hm * ATTN_HEAD_DIM ** -0.5) for hm in hms], axis=0).astype(BF16)
    s = _nt_dot(qs, k) + jnp.concatenate([bias] * ATTN_HEADS, axis=0)
    mx = jnp.max(s, axis=-1, keepdims=True)
    p = jnp.exp(s - mx)
    den = jnp.sum(p, axis=-1, keepdims=True)
    o4 = jnp.dot((p / den).astype(BF16), v, preferred_element_type=F32)
    l4 = mx + jnp.log(den)
    o_acc = jnp.zeros(o_ref.shape, F32)
    l_acc = jnp.zeros(l_ref.shape, F32)
    for h, hm in enumerate(hms):
        o_acc = o_acc + o4[h * qb:(h + 1) * qb, :] * hm
        l_acc = l_acc + l4[h * qb:(h + 1) * qb, :] * hm
    o_ref[...] = o_acc
    l_ref[...] = l_acc


def _band_attn(q_a, kv_a, batch, seq, g, window, dilation):
    lat = seq // dilation
    nb = lat // ATTN_BLOCK
    gw = GROUP_WIDTH
    qv = q_a.reshape(batch, lat, dilation * N_GROUPS * gw)
    kvv = kv_a.reshape(batch, lat, dilation * 2 * N_GROUPS * gw)
    blk = (None, ATTN_BLOCK, gw)
    cur = lambda off, per: (lambda b, r, j: (b, j, r * per + off))
    prev = lambda off, per: (lambda b, r, j: (b, jnp.maximum(j - 1, 0), r * per + off))
    o, l = pl.pallas_call(
        functools.partial(_band_attn_kernel, window // dilation),
        grid=(batch, dilation, nb),
        in_specs=[pl.BlockSpec(blk, cur(g, N_GROUPS)),
                  pl.BlockSpec(blk, cur(g, 2 * N_GROUPS)),
                  pl.BlockSpec(blk, prev(g, 2 * N_GROUPS)),
                  pl.BlockSpec(blk, cur(N_GROUPS + g, 2 * N_GROUPS)),
                  pl.BlockSpec(blk, prev(N_GROUPS + g, 2 * N_GROUPS))],
        out_specs=[pl.BlockSpec(blk, lambda b, r, j: (b, j, r)),
                   pl.BlockSpec(blk, lambda b, r, j: (b, j, r))],
        out_shape=[SDS((batch, lat, dilation * gw), F32)] * 2,
        compiler_params=_cp("parallel", "parallel", "arbitrary"),
        name=f"band_attn_d{dilation}",
    )(qv, kvv, kvv, kvv, kvv)
    return o.reshape(batch * seq, gw), l.reshape(batch * seq, gw)


def _cache_attn_kernel(wband, dilation, c_ref, q_ref, kv_ref, o_ref, l_ref, nc_ref):
    lw = c_ref.shape[-1]
    t_new = q_ref.shape[1]
    dh = q_ref.shape[2]

    def band_bias(n_keys, key_base):
        t = lax.broadcasted_iota(I32, (t_new, n_keys), 0)
        p = lax.broadcasted_iota(I32, (t_new, n_keys), 1) + key_base
        delta = lw + t - p
        ok = (jnp.where(delta >= 0, 1, 0) * jnp.where((delta & (dilation - 1)) == 0, 1, 0)
              * jnp.where(delta <= wband * dilation, 1, 0))
        return jnp.where(ok > 0, 0.0, NEG_INF)

    bias_c = band_bias(lw, 0)
    bias_n = band_bias(t_new, lw)
    place = (lax.broadcasted_iota(I32, (t_new, LANES), 1)
             == lax.broadcasted_iota(I32, (t_new, LANES), 0) + (LANES - t_new)).astype(F32)
    tail = lax.broadcasted_iota(I32, (1, LANES), 1) >= LANES - t_new
    for h in range(ATTN_HEADS):
        qh = (q_ref[h] * dh ** -0.5).astype(BF16)
        kt, vt = c_ref[0, h], c_ref[1, h]
        knt, vnt = kv_ref[0, h], kv_ref[1, h]
        sc = jnp.dot(qh, kt.astype(BF16), preferred_element_type=F32) + bias_c
        sn = jnp.dot(qh, knt.astype(BF16), preferred_element_type=F32) + bias_n
        mx = jnp.maximum(jnp.max(sc, axis=-1, keepdims=True), jnp.max(sn, axis=-1, keepdims=True))
        pc = jnp.exp(sc - mx)
        pn = jnp.exp(sn - mx)
        den = jnp.sum(pc, axis=-1, keepdims=True) + jnp.sum(pn, axis=-1, keepdims=True)
        o_ref[h] = (_nt_dot((pc / den).astype(BF16), vt.astype(BF16))
                    + _nt_dot((pn / den).astype(BF16), vnt.astype(BF16)))
        l_ref[h] = jnp.broadcast_to(mx + jnp.log(den), (t_new, dh))
        for kv, (old, new) in enumerate(((kt, knt), (vt, vnt))):
            rolled = pltpu.roll(old, lw - t_new, axis=1)
            new_tile = jnp.dot(new, place, precision=HIGHEST, preferred_element_type=F32)
            if lw > LANES:
                nc_ref[kv, h, :, 0:lw - LANES] = rolled[:, 0:lw - LANES]
            nc_ref[kv, h, :, lw - LANES:lw] = jnp.where(tail, new_tile, rolled[:, lw - LANES:lw])


def _cache_attn(q_s, kv_s, cache, g, window, dilation):
    b, lw = cache.shape[0], cache.shape[1]
    t_new = q_s.shape[1]
    nh, dh = ATTN_HEADS, ATTN_HEAD_DIM
    assert lw == window and lw % LANES == 0 and t_new % 8 == 0 and dilation & (dilation - 1) == 0
    ct = jnp.transpose(cache, (0, 2, 3, 4, 1))
    qh = jnp.transpose(q_s.reshape(b, t_new, N_GROUPS, nh, dh)[:, :, g], (0, 2, 1, 3))
    kvt = jnp.transpose(kv_s.reshape(b, t_new, 2, N_GROUPS, nh, dh)[:, :, :, g], (0, 2, 3, 4, 1))
    win = pl.BlockSpec((None, 2, nh, dh, lw), lambda i: (i, 0, 0, 0, 0))
    per_q = pl.BlockSpec((None, nh, t_new, dh), lambda i: (i, 0, 0, 0))
    o, l, nc = pl.pallas_call(
        functools.partial(_cache_attn_kernel, window // dilation, dilation),
        grid=(b,),
        in_specs=[win, per_q, pl.BlockSpec((None, 2, nh, dh, t_new), lambda i: (i, 0, 0, 0, 0))],
        out_specs=[per_q, per_q, win],
        out_shape=[SDS((b, nh, t_new, dh), F32), SDS((b, nh, t_new, dh), F32), SDS((b, 2, nh, dh, lw), F32)],
        compiler_params=_cp("parallel"),
        name=f"cache_attn_d{dilation}",
    )(ct, qh, kvt)
    tok = lambda a: jnp.transpose(a, (0, 2, 1, 3)).reshape(b * t_new, nh * dh)
    return tok(o), tok(l), jnp.transpose(nc, (0, 4, 1, 2, 3))


def _log_sigmoid(x):
    return jnp.minimum(x, 0.0) - jnp.log1p(jnp.exp(-jnp.abs(x)))


def _mlstm_kernel(n_valid, q_ref, k_ref, v_ref, o_ref, g_ref, bif_ref, ng_ref, c0_ref, n0_ref, m0_ref,
                  h_ref, c1_ref, n1_ref, m1_ref, c_sc, n_sc, m_sc):
    ci = pl.program_id(1)
    nh, dk, dv = MLSTM_HEADS, MLSTM_DK, MLSTM_DV
    L = MLSTM_CHUNK

    @pl.when(ci == 0)
    def _():
        c_sc[...] = jnp.zeros(c_sc.shape, F32)
        n_sc[...] = jnp.zeros(n_sc.shape, F32)
        m_sc[...] = jnp.zeros(m_sc.shape, F32)
        c_sc[:, 0:dk, :] = c0_ref[...]
        n_sc[:, :, 0:dk] = n0_ref[...]
        m_sc[...] = m0_ref[...]

    def rows(ref):
        x = ref[...].astype(F32)
        if n_valid < L:
            x = jnp.concatenate([x, jnp.zeros((L - n_valid, x.shape[1]), F32)], axis=0)
        return x

    q = rows(q_ref).astype(BF16)
    k = rows(k_ref)
    v = rows(v_ref).astype(BF16)
    og = rows(o_ref)
    graw = rows(g_ref) + bif_ref[...]
    lane = lax.broadcasted_iota(I32, (L, LANES), 1)
    row = lax.broadcasted_iota(I32, (L, LANES), 0)
    is_i = lane < nh
    is_f = (lane >= nh) & (lane < 2 * nh)
    live = row < n_valid
    gi = jnp.where(is_i, jnp.where(live, graw, NEG_INF), 0.0)
    gf = jnp.where(is_f & live, _log_sigmoid(graw), 0.0)
    tril = (lax.broadcasted_iota(I32, (L, L), 0) >= lax.broadcasted_iota(I32, (L, L), 1))
    bcol = jnp.dot(tril.astype(F32), gf, precision=HIGHEST, preferred_element_type=F32)
    brow = bcol.T
    irow = gi.T
    scale = dk ** -0.5

    def heads(x):
        return jnp.stack([x[:, h * LANES:(h + 1) * LANES] for h in range(nh)], axis=0)

    def bdot(a, b, ca, cb):
        return lax.dot_general(a, b, (((ca,), (cb,)), ((0,), (0,))), preferred_element_type=F32)

    q3, v3 = heads(q), heads(v)
    k3 = heads(k)
    k3b = k3.astype(BF16)
    b_c = jnp.stack([bcol[:, nh + h:nh + h + 1] for h in range(nh)], axis=0)
    b_r = jnp.stack([brow[nh + h:nh + h + 1, :] for h in range(nh)], axis=0)
    i_r = jnp.stack([irow[h:h + 1, :] for h in range(nh)], axis=0)
    m_prev = m_sc[...]
    c3 = c_sc[...]
    n3 = n_sc[...]
    dlog = jnp.where(tril[None], b_c - b_r + i_r, NEG_INF)
    m_inter = b_c + m_prev
    m_t = jnp.maximum(m_inter, jnp.max(dlog, axis=-1, keepdims=True))
    sc = bdot(q3, k3b, 2, 2) * scale * jnp.exp(dlog - m_t)
    inter = jnp.exp(m_inter - m_t)
    num = bdot(sc.astype(BF16), v3, 2, 1) + inter * (bdot(q3, c3.astype(BF16), 2, 1) * scale)
    qn = jnp.sum(q3.astype(F32) * n3, axis=-1, keepdims=True) * scale
    den = jnp.sum(sc, axis=-1, keepdims=True) + inter * qn
    hh = num / jnp.maximum(jnp.abs(den), jnp.exp(-m_t))
    b_last = b_c[:, L - 1:L, :]
    g_r = b_last - b_r + i_r
    m_new = jnp.maximum(b_last + m_prev, jnp.max(g_r, axis=-1, keepdims=True))
    ws = jnp.exp(g_r - m_new)
    decay = jnp.exp(b_last + m_prev - m_new)
    kw = (jnp.stack([k3[h].T for h in range(nh)], axis=0) * ws).astype(BF16)
    c_sc[...] = decay * c3 + bdot(kw, v3, 2, 1)
    ws8 = jnp.broadcast_to(ws, (nh, 8, L)).astype(BF16)
    n_sc[...] = decay * n3 + bdot(ws8, k3b, 2, 1)[:, 0:1, :]
    m_sc[...] = m_new
    mu = jnp.mean(hh, axis=-1, keepdims=True)
    var = jnp.mean(jnp.square(hh - mu), axis=-1, keepdims=True)
    out = jax.nn.sigmoid(heads(og)) * ((hh - mu) * lax.rsqrt(var + EPS) * heads(ng_ref[...]))
    for h in range(nh):
        h_ref[:, h * LANES:(h + 1) * LANES] = out[h, 0:n_valid, :].astype(h_ref.dtype)

    @pl.when(ci == pl.num_programs(1) - 1)
    def _():
        c1_ref[...] = c_sc[:, 0:dk, :]
        n1_ref[...] = n_sc[:, :, 0:dk]
        m1_ref[...] = m_sc[...]


def _mlstm(qm, km, vm, om, gates, bif_pad, norm_g, c0, n0, m0, n_valid, out_dtype, name):
    b, s, w = qm.shape
    nh, dk, dv = MLSTM_HEADS, MLSTM_DK, MLSTM_DV
    nc = s // n_valid
    assert s % n_valid == 0 and (n_valid == MLSTM_CHUNK or nc == 1)
    tok = lambda width: pl.BlockSpec((None, n_valid, width), lambda i, c: (i, c, 0))
    const = lambda shape: pl.BlockSpec(shape, lambda i, c: (0,) * len(shape))
    h, c1, n1, m1 = pl.pallas_call(
        functools.partial(_mlstm_kernel, n_valid),
        grid=(b, nc),
        in_specs=[tok(w), tok(w), tok(w), tok(w), tok(LANES), const((1, LANES)), const((1, w)),
                  pl.BlockSpec((None, nh, dk, dv), lambda i, c: (i, 0, 0, 0)),
                  pl.BlockSpec((None, nh, 1, dk), lambda i, c: (i, 0, 0, 0)),
                  pl.BlockSpec((None, nh, 1, 1), lambda i, c: (i, 0, 0, 0))],
        out_specs=[tok(w),
                   pl.BlockSpec((None, nh, dk, dv), lambda i, c: (i, 0, 0, 0)),
                   pl.BlockSpec((None, nh, 1, dk), lambda i, c: (i, 0, 0, 0)),
                   pl.BlockSpec((None, nh, 1, 1), lambda i, c: (i, 0, 0, 0))],
        out_shape=[SDS((b, s, w), out_dtype), SDS((b, nh, dk, dv), F32), SDS((b, nh, 1, dk), F32),
                   SDS((b, nh, 1, 1), F32)],
        scratch_shapes=[pltpu.VMEM((nh, DK_PAD, dv), F32), pltpu.VMEM((nh, 1, DK_PAD), F32),
                        pltpu.VMEM((nh, 1, 1), F32)],
        compiler_params=_cp("parallel", "arbitrary"),
        name=name,
    )(qm, km, vm, om, gates, bif_pad, norm_g, c0, n0.reshape(b, nh, 1, dk), m0.reshape(b, nh, 1, 1))
    return h, c1, n1.reshape(b, nh, dk), m1.reshape(b, nh)


def _pack_bf16_pairs(x):
    w = x.shape[1] // 2
    bits = lax.bitcast_convert_type(x.astype(BF16).astype(F32), U32)
    return (bits[:, :w] >> 16) | (bits[:, w:] & jnp.uint32(0xFFFF0000))


def _unpack_bf16_pairs(p):
    lo = lax.bitcast_convert_type(p << 16, F32).astype(BF16)
    hi = lax.bitcast_convert_type(p & jnp.uint32(0xFFFF0000), F32).astype(BF16)
    return lo, hi


def _merge_kernel(x_ref, sc1_ref, sh1_ref, g1_ref, sc2_ref, sh2_ref,
                  o0_ref, o1_ref, o2_ref, l0_ref, l1_ref, l2_ref, mo_ref,
                  wg_ref, wpa_ref, wpm_ref, wout_ref, lng_ref, lnb_ref, x1_ref, h2p_ref):
    x = x_ref[...]
    d = x.shape[1]
    h = (x * (1.0 + sc1_ref[0]) + sh1_ref[0]).astype(BF16)
    g = jnp.dot(h, wg_ref[...], preferred_element_type=F32)
    l0, l1, l2 = l0_ref[...], l1_ref[...], l2_ref[...]
    lm = jnp.maximum(jnp.maximum(l0, l1), l2)
    e0, e1, e2 = jnp.exp(l0 - lm), jnp.exp(l1 - lm), jnp.exp(l2 - lm)
    ao = (e0 * o0_ref[...] + e1 * o1_ref[...] + e2 * o2_ref[...]) / (e0 + e1 + e2)
    pa = jnp.dot(ao.astype(BF16), wpa_ref[...], preferred_element_type=F32)
    pm = jnp.dot(mo_ref[...].astype(BF16), wpm_ref[...], preferred_element_type=F32)
    merged = jax.nn.sigmoid(g[:, :d]) * pa + jax.nn.sigmoid(g[:, d:]) * pm
    y = jnp.dot(merged.astype(BF16), wout_ref[...], preferred_element_type=F32)
    x1 = _layer_norm(ALPHA * x + g1_ref[0] * y, lng_ref[...], lnb_ref[...])
    x1_ref[...] = x1
    h2p_ref[...] = _pack_bf16_pairs(x1 * (1.0 + sc2_ref[0]) + sh2_ref[0])


def _merge(x, mods, attn, mo, wg, wpa, wpm, wout, lng, lnb, tm, tiles_per_batch, name):
    n, d = x.shape
    tok = lambda width: pl.BlockSpec((tm, width), lambda i: (i, 0))
    const = lambda a: pl.BlockSpec(a.shape, lambda i: (0,) * a.ndim)
    return pl.pallas_call(
        _merge_kernel,
        grid=(n // tm,),
        in_specs=[tok(d)] + [_mod_spec(m, tm, tiles_per_batch) for m in mods]
        + [tok(GROUP_WIDTH)] * 6 + [tok(mo.shape[1])]
        + [const(a) for a in (wg, wpa, wpm, wout, lng, lnb)],
        out_specs=[tok(d), tok(d // 2)],
        out_shape=[SDS((n, d), F32), SDS((n, d // 2), U32)],
        compiler_params=_cp("parallel"),
        name=name,
    )(x, *mods, *attn, mo, wg, wpa, wpm, wout, lng, lnb)


def _first_index_of_max(x, idx, big):
    mx = jnp.max(x, axis=0, keepdims=True)
    first = jnp.min(jnp.where(x == mx, idx, big), axis=0, keepdims=True)
    return mx, first


def _route_kernel(hp_ref, wlo_ref, whi_ref, bias_ref, idx_ref, w_ref, rank_ref, cnt_ref):
    tm = hp_ref.shape[0]
    ne = N_EXPERTS
    per = ne // N_EXPERT_GROUPS
    lo, hi = _unpack_bf16_pairs(hp_ref[...])
    logits = _nt_dot(wlo_ref[...], lo) + _nt_dot(whi_ref[...], hi)
    scores = jax.nn.sigmoid(logits)
    sel = scores + bias_ref[...]
    erow = lax.broadcasted_iota(I32, (ne, tm), 0).astype(F32)
    prow = lax.broadcasted_iota(I32, (per, tm), 0).astype(F32)
    gs = []
    for g in range(N_EXPERT_GROUPS):
        xg = sel[g * per:(g + 1) * per, :]
        m1, i1 = _first_index_of_max(xg, prow, per)
        m2 = jnp.max(jnp.where(prow == i1, NEG_INF, xg), axis=0, keepdims=True)
        gs.append(m1 + m2)
    gs = jnp.concatenate(gs, axis=0)
    grow = lax.broadcasted_iota(I32, gs.shape, 0).astype(F32)
    gkeep = jnp.zeros(gs.shape, F32)
    for _ in range(TOPK_GROUPS):
        _, gi = _first_index_of_max(gs, grow, N_EXPERT_GROUPS)
        pick = grow == gi
        gkeep = jnp.where(pick, 1.0, gkeep)
        gs = jnp.where(pick, NEG_INF, gs)
    keep = jnp.concatenate([jnp.broadcast_to(gkeep[g:g + 1, :], (per, tm)) for g in range(N_EXPERT_GROUPS)], axis=0)
    cand = jnp.where(keep > 0, sel, NEG_INF)
    member = jnp.zeros((ne, tm), F32)
    picks, idxs, ws = [], [], []
    for _ in range(TOP_K):
        _, ei = _first_index_of_max(cand, erow, ne)
        pick = erow == ei
        picks.append(pick)
        idxs.append(ei)
        ws.append(jnp.sum(jnp.where(pick, scores, 0.0), axis=0, keepdims=True))
        cand = jnp.where(pick, NEG_INF, cand)
        member = jnp.where(pick, 1.0, member)
    wsum = ws[0]
    for wk in ws[1:]:
        wsum = wsum + wk
    idx_ref[...] = jnp.concatenate(idxs, axis=0).astype(I32)
    w_ref[...] = jnp.concatenate(ws, axis=0) / wsum * ROUTED_SCALE
    ti = lax.broadcasted_iota(I32, (tm, tm), 0)
    tj = lax.broadcasted_iota(I32, (tm, tm), 1)
    before = (ti < tj).astype(BF16)
    mb = member.astype(BF16)
    prefix = jnp.dot(mb, before, preferred_element_type=F32)
    rank_ref[...] = jnp.concatenate(
        [jnp.sum(jnp.where(p, prefix, 0.0), axis=0, keepdims=True) for p in picks], axis=0).astype(I32)
    cnt_ref[...] = _nt_dot(jnp.ones((8, tm), BF16), mb).astype(I32)


def _route(h2p, w_router_t, bias_col):
    n, half = h2p.shape
    tm = ROUTE_TILE
    nt = n // tm
    wlo = w_router_t[:, :half]
    whi = w_router_t[:, half:]
    pair = lambda dt: SDS((TOP_K, n), dt)
    return pl.pallas_call(
        _route_kernel,
        grid=(nt,),
        in_specs=[pl.BlockSpec((tm, half), lambda i: (i, 0)),
                  pl.BlockSpec(wlo.shape, lambda i: (0, 0)),
                  pl.BlockSpec(whi.shape, lambda i: (0, 0)),
                  pl.BlockSpec(bias_col.shape, lambda i: (0, 0))],
        out_specs=[pl.BlockSpec((TOP_K, tm), lambda i: (0, i))] * 3
        + [pl.BlockSpec((None, 8, N_EXPERTS), lambda i: (i, 0, 0))],
        out_shape=[pair(I32), pair(F32), pair(I32), SDS((nt, 8, N_EXPERTS), I32)],
        compiler_params=_cp("parallel"),
        name="route",
    )(h2p, wlo, whi, bias_col)


def _dest_kernel(idx_ref, rank_ref, base_ref, dest_ref):
    tm = idx_ref.shape[1]
    erow = lax.broadcasted_iota(I32, (N_EXPERTS, tm), 0)
    base = base_ref[...]
    idx = idx_ref[...]
    rows = [jnp.sum(jnp.where(erow == idx[k:k + 1, :], base, 0.0), axis=0, keepdims=True) for k in range(TOP_K)]
    dest_ref[...] = jnp.concatenate(rows, axis=0).astype(I32) + rank_ref[...]


def _dest(idx, rank, base_cols, tile):
    n = idx.shape[1]
    per_route = ROUTE_TILE // tile
    return pl.pallas_call(
        _dest_kernel,
        grid=(n // tile,),
        in_specs=[pl.BlockSpec((TOP_K, tile), lambda i: (0, i)),
                  pl.BlockSpec((TOP_K, tile), lambda i: (0, i)),
                  pl.BlockSpec((None, N_EXPERTS, 1), lambda i: (i // per_route, 0, 0))],
        out_specs=pl.BlockSpec((None, TOP_K, tile), lambda i: (i, 0, 0)),
        out_shape=SDS((n // tile, TOP_K, tile), I32),
        compiler_params=_cp("parallel"),
        name="dest",
    )(idx, rank, base_cols)


def _row_copy_loop(tile, make_copy, start):
    def body(t, carry):
        for k in range(TOP_K):
            cp = make_copy(k, t)
            if start:
                cp.start(priority=k % 2)
            else:
                cp.wait()
        return carry
    lax.fori_loop(0, tile, body, 0, unroll=8)


def _dest_fetch(dest_hbm, tile_idx, dest_smem, sem_idx):
    n_idx = dest_smem.shape[0]
    start = tile_idx * n_idx
    if not isinstance(start, int):
        start = pl.multiple_of(start, n_idx)
    return pltpu.make_async_copy(dest_hbm.at[pl.ds(start, n_idx)], dest_smem, sem_idx)


def _scatter_kernel(dest_hbm, hp_ref, xs_in, xs_out, stage, dest_smem, sem_idx, sem_rows):
    del xs_in
    i = pl.program_id(0)
    n = pl.num_programs(0)
    tile = hp_ref.shape[0]
    slot = lax.rem(i, 2)
    idx_cp = _dest_fetch(dest_hbm, i, dest_smem, sem_idx)
    idx_cp.start()

    def row_copy(k, t):
        return pltpu.make_async_copy(stage.at[slot, pl.ds(t, 1)], xs_out.at[pl.ds(dest_smem[k * tile + t], 1)],
                                     sem_rows.at[slot])

    def wait_tile(s):
        _row_copy_loop(tile, lambda k, t: pltpu.make_async_copy(
            stage.at[s, pl.ds(0, 1)], xs_out.at[pl.ds(0, 1)], sem_rows.at[s]), False)

    @pl.when(i >= 2)
    def _():
        wait_tile(slot)

    stage[slot] = hp_ref[...]
    idx_cp.wait()
    _row_copy_loop(tile, row_copy, True)

    @pl.when(i == n - 1)
    def _():
        @pl.when(n >= 2)
        def _():
            wait_tile(1 - slot)
        wait_tile(slot)


def _scatter_rows(dest_flat, h2p, n_rows):
    n, half = h2p.shape
    tile = MOE_TILE
    xs0 = jnp.zeros((n_rows, half), U32)
    return pl.pallas_call(
        _scatter_kernel,
        grid=(n // tile,),
        in_specs=[pl.BlockSpec(memory_space=pl.ANY),
                  pl.BlockSpec((tile, half), lambda i: (i, 0)),
                  pl.BlockSpec(memory_space=pl.ANY)],
        out_specs=pl.BlockSpec(memory_space=pl.ANY),
        out_shape=SDS((n_rows, half), U32),
        scratch_shapes=[pltpu.VMEM((2, tile, half), U32), pltpu.SMEM((TOP_K * tile,), I32),
                        pltpu.SemaphoreType.DMA, pltpu.SemaphoreType.DMA((2,))],
        input_output_aliases={2: 0},
        compiler_params=_cp("arbitrary"),
        name="scatter_rows",
    )(dest_flat, h2p, xs0)


def _expert_kernel(blk_exp_ref, n_used_ref, xs_ref, win_ref, wout_ref, ys_ref, win_sc, wout_sc):
    i = pl.program_id(0)
    prev = blk_exp_ref[jnp.maximum(i - 1, 0)]

    @pl.when((i == 0) | (blk_exp_ref[i] != prev))
    def _():
        win_sc[...] = win_ref[...].astype(BF16)
        wout_sc[...] = wout_ref[...].astype(BF16)

    @pl.when(i < n_used_ref[0])
    def _():
        half = xs_ref.shape[1]
        lo, hi = _unpack_bf16_pairs(xs_ref[...])
        ag = (jnp.dot(lo, win_sc[0:half, :], preferred_element_type=F32)
              + jnp.dot(hi, win_sc[half:2 * half, :], preferred_element_type=F32))
        a = ag[:, :D_EXPERT]
        g = ag[:, D_EXPERT:]
        mid = (a * jax.nn.sigmoid(a) * g).astype(BF16)
        ys_ref[...] = jnp.dot(mid, wout_sc[...], preferred_element_type=F32)

    @pl.when(i >= n_used_ref[0])
    def _():
        ys_ref[...] = jnp.zeros(ys_ref.shape, F32)


def _experts(xs, blk_exp, n_used, w_exp_in, w_exp_out):
    n_rows, half = xs.shape
    d = 2 * half
    nb = n_rows // MOE_BLOCK
    grid_spec = pltpu.PrefetchScalarGridSpec(
        num_scalar_prefetch=2,
        grid=(nb,),
        in_specs=[pl.BlockSpec((MOE_BLOCK, half), lambda i, be, nu: (i, 0)),
                  pl.BlockSpec((None, d, 2 * D_EXPERT), lambda i, be, nu: (be[i], 0, 0)),
                  pl.BlockSpec((None, D_EXPERT, d), lambda i, be, nu: (be[i], 0, 0))],
        out_specs=pl.BlockSpec((MOE_BLOCK, d), lambda i, be, nu: (i, 0)),
        scratch_shapes=[pltpu.VMEM((d, 2 * D_EXPERT), BF16), pltpu.VMEM((D_EXPERT, d), BF16)],
    )
    return pl.pallas_call(
        _expert_kernel,
        grid_spec=grid_spec,
        out_shape=SDS((n_rows, d), F32),
        compiler_params=_cp("arbitrary"),
        name="experts",
    )(blk_exp, n_used, xs, w_exp_in, w_exp_out)


def _final_kernel(dest_hbm, x1_ref, hp_ref, w_ref, g2_ref, ys_hbm, wsin_ref, wsout_ref, lng_ref, lnb_ref,
                  y_ref, rows_sc, dest_smem, sem_idx, sem_rows, *, tile_offset):
    i = pl.program_id(0)
    n = pl.num_programs(0)
    tile = x1_ref.shape[0]
    slot = lax.rem(i, 2)

    def start_rows(s):
        _row_copy_loop(tile, lambda k, t: pltpu.make_async_copy(
            ys_hbm.at[pl.ds(dest_smem[k * tile + t], 1)], rows_sc.at[s, k, pl.ds(t, 1)], sem_rows.at[s]), True)

    @pl.when(i == 0)
    def _():
        first = _dest_fetch(dest_hbm, tile_offset, dest_smem, sem_idx)
        first.start()
        first.wait()
        start_rows(0)

    nxt = _dest_fetch(dest_hbm, jnp.minimum(i + 1, n - 1) + tile_offset, dest_smem, sem_idx)

    @pl.when(i + 1 < n)
    def _():
        nxt.start()

    half = hp_ref.shape[1]
    lo, hi = _unpack_bf16_pairs(hp_ref[...])
    ag = (jnp.dot(lo, wsin_ref[0:half, :], preferred_element_type=F32)
          + jnp.dot(hi, wsin_ref[half:2 * half, :], preferred_element_type=F32))
    a = ag[:, :D_SHARED]
    g = ag[:, D_SHARED:]
    y2 = jnp.dot((a * jax.nn.sigmoid(a) * g).astype(BF16), wsout_ref[...], preferred_element_type=F32)
    w_cols = jnp.concatenate([w_ref[...], jnp.zeros((tile - TOP_K, tile), F32)], axis=0).T

    @pl.when(i + 1 < n)
    def _():
        nxt.wait()
        start_rows(1 - slot)

    _row_copy_loop(tile, lambda k, t: pltpu.make_async_copy(
        ys_hbm.at[pl.ds(0, 1)], rows_sc.at[slot, 0, pl.ds(0, 1)], sem_rows.at[slot]), False)
    for k in range(TOP_K):
        y2 = y2 + w_cols[:, k:k + 1] * rows_sc[slot, k]
    y_ref[...] = _layer_norm(ALPHA * x1_ref[...] + g2_ref[0] * y2, lng_ref[...], lnb_ref[...])


def _final(dest_flat, x1, h2p, w_top, g2_3, ys, wsin, wsout, lng, lnb, tile_offset, tiles_per_batch, name):
    n, d = x1.shape
    tile = MOE_TILE
    assert tile == LANES
    const = lambda a: pl.BlockSpec(a.shape, lambda i: (0,) * a.ndim)
    return pl.pallas_call(
        functools.partial(_final_kernel, tile_offset=tile_offset),
        grid=(n // tile,),
        in_specs=[pl.BlockSpec(memory_space=pl.ANY),
                  pl.BlockSpec((tile, d), lambda i: (i, 0)),
                  pl.BlockSpec((tile, d // 2), lambda i: (i, 0)),
                  pl.BlockSpec((TOP_K, tile), lambda i: (0, i)),
                  _mod_spec(g2_3, tile, tiles_per_batch),
                  pl.BlockSpec(memory_space=pl.ANY),
                  const(wsin), const(wsout), const(lng), const(lnb)],
        out_specs=pl.BlockSpec((tile, d), lambda i: (i, 0)),
        out_shape=SDS((n, d), F32),
        scratch_shapes=[pltpu.VMEM((2, TOP_K, tile, d), F32), pltpu.SMEM((TOP_K * tile,), I32),
                        pltpu.SemaphoreType.DMA, pltpu.SemaphoreType.DMA((2,))],
        compiler_params=_cp("arbitrary"),
        name=name,
    )(dest_flat, x1, h2p, w_top, g2_3, ys, wsin, wsout, lng, lnb)


def _sorted_layout(counts):
    totals = jnp.sum(counts, axis=0)
    padded = (totals + MOE_BLOCK - 1) // MOE_BLOCK * MOE_BLOCK
    pends = jnp.cumsum(padded)
    base = (pends - padded)[None, :] + jnp.cumsum(counts, axis=0) - counts
    return base.astype(I32), pends


def _moe_routed(h2p, w_router, router_bias, w_exp_in, w_exp_out):
    n = h2p.shape[0]
    idx, w_top, rank, cnt = _route(h2p, w_router.T.astype(BF16), router_bias.reshape(N_EXPERTS, 1))
    base, pends = _sorted_layout(cnt[:, 0, :])
    n_blocks = -(-(n * TOP_K + N_EXPERTS * (MOE_BLOCK - 1)) // MOE_BLOCK)
    blk_start = jnp.arange(n_blocks, dtype=I32) * MOE_BLOCK
    blk_exp = jnp.minimum(jnp.sum((pends[None, :] <= blk_start[:, None]).astype(I32), axis=1), N_EXPERTS - 1)
    n_used = (pends[-1:] // MOE_BLOCK).astype(I32)
    dest = _dest(idx, rank, base.astype(F32)[:, :, None], MOE_TILE).reshape(-1)
    xs = _scatter_rows(dest, h2p, n_blocks * MOE_BLOCK)
    ys = _experts(xs, blk_exp, n_used, w_exp_in, w_exp_out)
    return dest, w_top, ys


def _pad_heads(w, nh, dk):
    d = w.shape[0]
    return jnp.pad(w.reshape(d, nh, dk), ((0, 0), (0, 0), (0, DK_PAD - dk))).reshape(d, nh * DK_PAD)


def _split_w_in(w_in):
    aw = N_GROUPS * GROUP_WIDTH
    nh, dk, dv = MLSTM_HEADS, MLSTM_DK, MLSTM_DV
    offs = np.cumsum([0, aw, aw, aw, nh * dk, nh * dk, nh * dv, nh * dv, nh, nh, w_in.shape[0], w_in.shape[0]])
    seg = [w_in[:, offs[i]:offs[i + 1]] for i in range(11)]
    q_a, k_a, v_a, q_m, k_m, v_m, o_m, i_m, f_m, g_a, g_b = seg
    bf = lambda a: a.astype(BF16)
    w_gates = jnp.pad(jnp.concatenate([i_m, f_m], axis=1), ((0, 0), (0, LANES - 2 * nh)))
    return dict(qa=bf(q_a), kva=bf(jnp.concatenate([k_a, v_a], axis=1)),
                qm=bf(_pad_heads(q_m, nh, dk)), km=bf(_pad_heads(k_m, nh, dk)), vm=bf(v_m), om=bf(o_m),
                gates=w_gates, g=bf(jnp.concatenate([g_a, g_b], axis=1)))


def _mixing(x, mods, tm, tiles_per_batch, batch, seq, caches, states, wts, prompt):
    n, d = x.shape
    sc1, sh1, g1 = mods["scale1"], mods["shift1"], mods["gate1"]
    sc2, sh2 = mods["scale2"], mods["shift2"]
    act = BF16 if prompt else F32
    q_a, kv_a = _proj(x, sc1, sh1, [wts["qa"], wts["kva"]], [act, F32], None, tm, tiles_per_batch,
                      "proj_attn_p" if prompt else "proj_attn_s")
    qm, km, vm, om, gates = _proj(x, sc1, sh1, [wts["qm"], wts["km"], wts["vm"], wts["om"]],
                                  [act, act, act, F32], wts["gates"], tm, tiles_per_batch,
                                  "proj_mlstm_p" if prompt else "proj_mlstm_s")
    attn_o, attn_l, bufs = [], [], []
    for g, (window, dilation) in enumerate(ATTN_GROUPS):
        if prompt:
            o, l = _band_attn(q_a, kv_a, batch, seq, g, window, dilation)
            keep = min(window, seq)
            kv3 = kv_a.reshape(batch, seq, 2, N_GROUPS, GROUP_WIDTH)[:, seq - keep:, :, g, :]
            bufs.append(kv3.reshape(1, batch, keep, 2, ATTN_HEADS, ATTN_HEAD_DIM))
        else:
            o, l, nc = _cache_attn(q_a.reshape(batch, seq, -1), kv_a.reshape(batch, seq, -1), caches[g], g,
                                   window, dilation)
            bufs.append(nc[None])
        attn_o.append(o)
        attn_l.append(l)
    r3 = lambda a: a.reshape(batch, seq, a.shape[-1])
    n_valid = MLSTM_CHUNK if prompt else seq
    mo, c1, n1, m1 = _mlstm(r3(qm), r3(km), r3(vm), r3(om), r3(gates), wts["bif"], wts["norm_g"],
                            states[0], states[1], states[2], n_valid, act,
                            "mlstm_p" if prompt else "mlstm_s")
    x1, h2p = _merge(x, [sc1, sh1, g1, sc2, sh2], attn_o + attn_l, mo.reshape(n, -1),
                     wts["g"], wts["pa"], wts["pm"], wts["out"], wts["ln1_g"], wts["ln1_b"],
                     tm, tiles_per_batch, "merge_p" if prompt else "merge_s")
    return x1, h2p, bufs, (c1[None], n1[None], m1[None])


def _mod_pieces(mod, d, rows_per_batch, tm):
    names = ("shift1", "scale1", "gate1", "shift2", "scale2", "gate2")
    out = {}
    for p, name in enumerate(names):
        piece = mod[:, p * d:(p + 1) * d]
        if rows_per_batch % tm == 0:
            out[name] = piece[:, None, :]
        else:
            out[name] = jnp.repeat(piece, rows_per_batch, axis=0).reshape(-1, tm, d)
    return out


def kernel(x_prompt, x_sample, cache_kv_w128, cache_kv_w512, cache_kv_w2048, state_mlstm_C, state_mlstm_n, state_mlstm_m, c_prompt, c_sample, w_ada, b_ada, w_in, b_if, mlstm_norm_g, w_proj_attn, w_proj_mlstm, w_out, ln1_g, ln1_b, w_router, router_bias, w_exp_in, w_exp_out, w_sh_in, w_sh_out, ln2_g, ln2_b):
    assert w_ada.shape[0] == DEPTH
    bp, sp, d = x_prompt.shape
    bs, ss, _ = x_sample.shape
    nh = MLSTM_HEADS
    np_, ns = bp * sp, bs * ss

    mod = _ada(jnp.concatenate([c_prompt, c_sample], axis=0), w_ada[0], b_ada[0])
    wts = _split_w_in(w_in[0])
    wts.update(
        bif=jnp.pad(b_if[0], (0, LANES - 2 * nh)).reshape(1, LANES),
        norm_g=mlstm_norm_g[0].reshape(1, -1),
        pa=w_proj_attn[0].astype(BF16), pm=w_proj_mlstm[0].astype(BF16), out=w_out[0].astype(BF16),
        ln1_g=ln1_g[0].reshape(1, d), ln1_b=ln1_b[0].reshape(1, d))

    tm_p, tm_s = 512, 256
    mods_p = _mod_pieces(mod[:bp], d, sp, tm_p)
    mods_s = _mod_pieces(mod[bp:], d, ss, tm_s)
    zeros_p = (jnp.zeros((bp, nh, MLSTM_DK, MLSTM_DV), F32), jnp.zeros((bp, nh, MLSTM_DK), F32),
               jnp.zeros((bp, nh), F32))
    x1p, h2p_p, bufs_p, st_p = _mixing(x_prompt.reshape(np_, d), mods_p, tm_p, sp // tm_p, bp, sp, None,
                                       zeros_p, wts, True)
    caches = (cache_kv_w128[0], cache_kv_w512[0], cache_kv_w2048[0])
    states = (state_mlstm_C[0], state_mlstm_n[0], state_mlstm_m[0])
    x1s, h2p_s, bufs_s, st_s = _mixing(x_sample.reshape(ns, d), mods_s, tm_s, 1, bs, ss, caches, states, wts, False)

    h2p = jnp.concatenate([h2p_p, h2p_s], axis=0)
    dest, w_top, ys = _moe_routed(h2p, w_router[0], router_bias[0], w_exp_in[0], w_exp_out[0])
    wsin, wsout = w_sh_in[0].astype(BF16), w_sh_out[0].astype(BF16)
    lng, lnb = ln2_g[0].reshape(1, d), ln2_b[0].reshape(1, d)
    g2_p = _mod_pieces(mod[:bp], d, sp, MOE_TILE)["gate2"]
    g2_s = _mod_pieces(mod[bp:], d, ss, MOE_TILE)["gate2"]
    y_p = _final(dest, x1p, h2p_p, w_top[:, :np_], g2_p, ys, wsin, wsout, lng, lnb, 0, sp // MOE_TILE, "final_p")
    y_s = _final(dest, x1s, h2p_s, w_top[:, np_:], g2_s, ys, wsin, wsout, lng, lnb, np_ // MOE_TILE, 1, "final_s")

    return (y_p.reshape(bp, sp, d), y_s.reshape(bs, ss, d),
            bufs_p[0], bufs_p[1], bufs_p[2], st_p[0], st_p[1], st_p[2],
            bufs_s[0], bufs_s[1], bufs_s[2], st_s[0], st_s[1], st_s[2])
```

```python
import functools

import numpy as np
import jax
import jax.numpy as jnp
from jax import lax
from jax.experimental import pallas as pl
from jax.experimental.pallas import tpu as pltpu

F32 = jnp.float32
BF16 = jnp.bfloat16
I32 = jnp.int32
U32 = jnp.uint32
HIGHEST = lax.Precision.HIGHEST
SDS = jax.ShapeDtypeStruct
NEG_INF = float("-inf")

ATTN_GROUPS = ((128, 1), (512, 4), (2048, 16))
N_GROUPS = 3
ATTN_HEADS = 4
ATTN_HEAD_DIM = 64
GROUP_WIDTH = ATTN_HEADS * ATTN_HEAD_DIM
ATTN_BLOCK = 128
MLSTM_HEADS = 8
MLSTM_DK = 64
MLSTM_DV = 128
N_EXPERTS = 256
TOP_K = 8
N_EXPERT_GROUPS = 8
TOPK_GROUPS = 4
D_EXPERT = 256
D_SHARED = 256
ROUTED_SCALE = 2.5
DEPTH = 1
ALPHA = (2 * DEPTH) ** 0.25
EPS = 1e-5

LANES = 128
MLSTM_CHUNK = 128
DK_PAD = LANES
MOE_TILE = 128
MOE_BLOCK = 512
ROUTE_TILE = 512
VMEM_LIMIT = 56 << 20


def _cp(*sem, vmem=VMEM_LIMIT):
    return pltpu.CompilerParams(dimension_semantics=sem, vmem_limit_bytes=vmem)


def _nt_dot(a, b):
    return lax.dot_general(a, b, (((1,), (1,)), ((), ())), preferred_element_type=F32)


def _layer_norm(r, g, b):
    mu = jnp.mean(r, axis=-1, keepdims=True)
    var = jnp.mean(jnp.square(r - mu), axis=-1, keepdims=True)
    return (r - mu) * lax.rsqrt(var + EPS) * g + b


def _mod_spec(mod3, tm, tiles_per_batch):
    d = mod3.shape[-1]
    if mod3.shape[1] == 1:
        return pl.BlockSpec((1, 1, d), lambda i: (i // tiles_per_batch, 0, 0))
    return pl.BlockSpec((1, tm, d), lambda i: (i, 0, 0))


def _ada_kernel(c_ref, w_ref, b_ref, o_ref):
    c = c_ref[...]
    s = c * jax.nn.sigmoid(c)
    o_ref[...] = jnp.dot(s, w_ref[...], precision=HIGHEST, preferred_element_type=F32) + b_ref[...]


def _ada(c_all, w_ada, b_ada):
    r, d = c_all.shape
    n = w_ada.shape[1]
    tn = 1024
    return pl.pallas_call(
        _ada_kernel,
        grid=(n // tn,),
        in_specs=[pl.BlockSpec((r, d), lambda j: (0, 0)),
                  pl.BlockSpec((d, tn), lambda j: (0, j)),
                  pl.BlockSpec((1, tn), lambda j: (0, j))],
        out_specs=pl.BlockSpec((r, tn), lambda j: (0, j)),
        out_shape=SDS((r, n), F32),
        compiler_params=_cp("parallel"),
        name="ada",
    )(c_all, w_ada, b_ada.reshape(1, n))


def _proj_kernel(n_w, has_hp, x_ref, sc_ref, sh_ref, *refs):
    w_refs = refs[:n_w]
    pos = n_w
    if has_hp:
        whp_ref = refs[pos]
        pos += 1
    o_refs = refs[pos:pos + n_w]
    h = x_ref[...] * (1.0 + sc_ref[0]) + sh_ref[0]
    hb = h.astype(BF16)
    for w_ref, o_ref in zip(w_refs, o_refs):
        o_ref[...] = jnp.dot(hb, w_ref[...], preferred_element_type=F32).astype(o_ref.dtype)
    if has_hp:
        ohp_ref = refs[pos + n_w]
        ohp_ref[...] = jnp.dot(h, whp_ref[...], precision=HIGHEST, preferred_element_type=F32)


def _proj(x, sc3, sh3, weights, out_dtypes, w_hp, tm, tiles_per_batch, name):
    n, d = x.shape
    n_w = len(weights)
    has_hp = w_hp is not None
    in_specs = [pl.BlockSpec((tm, d), lambda i: (i, 0)),
                _mod_spec(sc3, tm, tiles_per_batch), _mod_spec(sh3, tm, tiles_per_batch)]
    in_specs += [pl.BlockSpec(w.shape, lambda i: (0, 0)) for w in weights]
    out_specs = [pl.BlockSpec((tm, w.shape[1]), lambda i: (i, 0)) for w in weights]
    out_shape = [SDS((n, w.shape[1]), dt) for w, dt in zip(weights, out_dtypes)]
    args = [x, sc3, sh3, *weights]
    if has_hp:
        in_specs.append(pl.BlockSpec(w_hp.shape, lambda i: (0, 0)))
        out_specs.append(pl.BlockSpec((tm, w_hp.shape[1]), lambda i: (i, 0)))
        out_shape.append(SDS((n, w_hp.shape[1]), F32))
        args.append(w_hp)
    return pl.pallas_call(
        functools.partial(_proj_kernel, n_w, has_hp),
        grid=(n // tm,),
        in_specs=in_specs, out_specs=out_specs, out_shape=out_shape,
        compiler_params=_cp("parallel"),
        name=name,
    )(*args)


def _head_masks(width):
    lane_head = lax.broadcasted_iota(I32, (1, width), 1) // ATTN_HEAD_DIM
    return [(lane_head == h).astype(F32) for h in range(ATTN_HEADS)]


def _band_attn_kernel(wband, q_ref, kc_ref, kp_ref, vc_ref, vp_ref, o_ref, l_ref):
    j = pl.program_id(2)
    qb = q_ref.shape[0]
    q = q_ref[...].astype(F32)
    k = jnp.concatenate([kp_ref[...], kc_ref[...]], axis=0).astype(BF16)
    v = jnp.concatenate([vp_ref[...], vc_ref[...]], axis=0).astype(BF16)
    qi = lax.broadcasted_iota(I32, (qb, 2 * qb), 0)
    kj = lax.broadcasted_iota(I32, (qb, 2 * qb), 1)
    dist = qi + qb - kj
    valid = jnp.where(dist >= 0, 1, 0) * jnp.where(dist <= wband, 1, 0) * jnp.where(j * qb + kj - qb >= 0, 1, 0)
    bias = jnp.where(valid > 0, 0.0, NEG_INF)
    hms = _head_masks(q.shape[1])
    qs = jnp.concatenate([q * (hm * ATTN_HEAD_DIM ** -0.5) for hm in hms], axis=0).astype(BF16)
    s = _nt_dot(qs, k) + jnp.concatenate([bias] * ATTN_HEADS, axis=0)
    mx = jnp.max(s, axis=-1, keepdims=True)
    p = jnp.exp(s - mx)
    den = jnp.sum(p, axis=-1, keepdims=True)
    o4 = jnp.dot((p / den).astype(BF16), v, preferred_element_type=F32)
    l4 = mx + jnp.log(den)
    o_acc = jnp.zeros(o_ref.shape, F32)
    l_acc = jnp.zeros(l_ref.shape, F32)
    for h, hm in enumerate(hms):
        o_acc = o_acc + o4[h * qb:(h + 1) * qb, :] * hm
        l_acc = l_acc + l4[h * qb:(h + 1) * qb, :] * hm
    o_ref[...] = o_acc
    l_ref[...] = l_acc


def _band_attn(q_l, k_l, v_l, window):
    batch, dilation, lat, gw = q_l.shape
    nb = lat // ATTN_BLOCK
    blk = (None, None, ATTN_BLOCK, gw)
    cur = lambda b, r, j: (b, r, j, 0)
    prev = lambda b, r, j: (b, r, jnp.maximum(j - 1, 0), 0)
    return pl.pallas_call(
        functools.partial(_band_attn_kernel, window // dilation),
        grid=(batch, dilation, nb),
        in_specs=[pl.BlockSpec(blk, cur), pl.BlockSpec(blk, cur), pl.BlockSpec(blk, prev),
                  pl.BlockSpec(blk, cur), pl.BlockSpec(blk, prev)],
        out_specs=[pl.BlockSpec(blk, cur), pl.BlockSpec(blk, cur)],
        out_shape=[SDS(q_l.shape, F32)] * 2,
        compiler_params=_cp("parallel", "parallel", "arbitrary"),
        name=f"band_attn_d{dilation}",
    )(q_l, k_l, k_l, v_l, v_l)


def _rows_to_lane_tiles(x, ref):
    for j in range(x.shape[1] // LANES):
        ref[j] = x[:, j * LANES:(j + 1) * LANES]


def _proj_lattice_kernel(dils, t0, x_ref, sc_ref, sh_ref, wq_ref, wkv_ref, *refs):
    ng = len(dils)
    lat_refs = refs[:3 * ng]
    kv_tail_ref = refs[3 * ng]
    zs = refs[3 * ng + 1]
    gw = GROUP_WIDTH
    tm = x_ref.shape[0]
    hb = (x_ref[...] * (1.0 + sc_ref[0]) + sh_ref[0]).astype(BF16)
    zq = jnp.dot(hb, wq_ref[...], preferred_element_type=F32)
    zkv = jnp.dot(hb, wkv_ref[...], preferred_element_type=F32)

    @pl.when(pl.program_id(1) >= t0)
    def _():
        kv_tail_ref[...] = zkv

    for g, d in enumerate(dils):
        for part, z in enumerate((zq[:, g * gw:(g + 1) * gw], zkv[:, g * gw:(g + 1) * gw],
                                  zkv[:, (ng + g) * gw:(ng + g + 1) * gw])):
            out = lat_refs[3 * g + part]
            if d == 1:
                out[0] = z.astype(out.dtype)
            else:
                _rows_to_lane_tiles(z, zs)
                for r in range(d):
                    out[r] = jnp.concatenate([zs[j, pl.ds(r, tm // d, stride=d), :] for j in range(gw // LANES)],
                                             axis=1).astype(out.dtype)


def _proj_lattice(x, sc3, sh3, wq, wkv, batch, seq, tm, tail):
    d_model = x.shape[1]
    tiles = seq // tm
    t0 = (seq - tail) // tm
    assert seq % tm == 0 and (seq - tail) % tm == 0
    dils = tuple(d for _, d in ATTN_GROUPS)
    gw = GROUP_WIDTH
    lat_specs, lat_shapes = [], []
    for d in dils:
        for _ in range(3):
            lat_specs.append(pl.BlockSpec((None, d, tm // d, gw), lambda b, t: (b, 0, t, 0)))
            lat_shapes.append(SDS((batch, d, seq // d, gw), BF16))
    outs = pl.pallas_call(
        functools.partial(_proj_lattice_kernel, dils, t0),
        grid=(batch, tiles),
        in_specs=[pl.BlockSpec((tm, d_model), lambda b, t: (b * tiles + t, 0)),
                  pl.BlockSpec((1, 1, d_model), lambda b, t: (b, 0, 0)),
                  pl.BlockSpec((1, 1, d_model), lambda b, t: (b, 0, 0)),
                  pl.BlockSpec(wq.shape, lambda b, t: (0, 0)),
                  pl.BlockSpec(wkv.shape, lambda b, t: (0, 0))],
        out_specs=lat_specs + [pl.BlockSpec((None, tm, wkv.shape[1]), lambda b, t: (b, jnp.maximum(t - t0, 0), 0))],
        out_shape=lat_shapes + [SDS((batch, tail, wkv.shape[1]), F32)],
        scratch_shapes=[pltpu.VMEM((gw // LANES, tm, LANES), F32)],
        compiler_params=_cp("parallel", "arbitrary"),
        name="proj_attn_p",
    )(x, sc3, sh3, wq, wkv)
    return [outs[3 * g:3 * g + 3] for g in range(len(dils))], outs[-1]


def _cache_attn_kernel(wband, dilation, c_ref, q_ref, kv_ref, o_ref, l_ref, nc_ref):
    lw = c_ref.shape[-1]
    t_new = q_ref.shape[1]
    dh = q_ref.shape[2]

    def band_bias(n_keys, key_base):
        t = lax.broadcasted_iota(I32, (t_new, n_keys), 0)
        p = lax.broadcasted_iota(I32, (t_new, n_keys), 1) + key_base
        delta = lw + t - p
        ok = (jnp.where(delta >= 0, 1, 0) * jnp.where((delta & (dilation - 1)) == 0, 1, 0)
              * jnp.where(delta <= wband * dilation, 1, 0))
        return jnp.where(ok > 0, 0.0, NEG_INF)

    bias_c = band_bias(lw, 0)
    bias_n = band_bias(t_new, lw)
    place = (lax.broadcasted_iota(I32, (t_new, LANES), 1)
             == lax.broadcasted_iota(I32, (t_new, LANES), 0) + (LANES - t_new)).astype(F32)
    tail = lax.broadcasted_iota(I32, (1, LANES), 1) >= LANES - t_new
    for h in range(ATTN_HEADS):
        qh = (q_ref[h] * dh ** -0.5).astype(BF16)
        kt, vt = c_ref[0, h], c_ref[1, h]
        knt, vnt = kv_ref[0, h], kv_ref[1, h]
        sc = jnp.dot(qh, kt.astype(BF16), preferred_element_type=F32) + bias_c
        sn = jnp.dot(qh, knt.astype(BF16), preferred_element_type=F32) + bias_n
        mx = jnp.maximum(jnp.max(sc, axis=-1, keepdims=True), jnp.max(sn, axis=-1, keepdims=True))
        pc = jnp.exp(sc - mx)
        pn = jnp.exp(sn - mx)
        den = jnp.sum(pc, axis=-1, keepdims=True) + jnp.sum(pn, axis=-1, keepdims=True)
        o_ref[h] = (_nt_dot((pc / den).astype(BF16), vt.astype(BF16))
                    + _nt_dot((pn / den).astype(BF16), vnt.astype(BF16)))
        l_ref[h] = jnp.broadcast_to(mx + jnp.log(den), (t_new, dh))
        for kv, (old, new) in enumerate(((kt, knt), (vt, vnt))):
            rolled = pltpu.roll(old, lw - t_new, axis=1)
            new_tile = jnp.dot(new, place, precision=HIGHEST, preferred_element_type=F32)
            if lw > LANES:
                nc_ref[kv, h, :, 0:lw - LANES] = rolled[:, 0:lw - LANES]
            nc_ref[kv, h, :, lw - LANES:lw] = jnp.where(tail, new_tile, rolled[:, lw - LANES:lw])


def _cache_attn(q_s, kv_s, cache, g, window, dilation):
    b, lw = cache.shape[0], cache.shape[1]
    t_new = q_s.shape[1]
    nh, dh = ATTN_HEADS, ATTN_HEAD_DIM
    assert lw == window and lw % LANES == 0 and t_new % 8 == 0 and dilation & (dilation - 1) == 0
    ct = jnp.transpose(cache, (0, 2, 3, 4, 1))
    qh = jnp.transpose(q_s.reshape(b, t_new, N_GROUPS, nh, dh)[:, :, g], (0, 2, 1, 3))
    kvt = jnp.transpose(kv_s.reshape(b, t_new, 2, N_GROUPS, nh, dh)[:, :, :, g], (0, 2, 3, 4, 1))
    win = pl.BlockSpec((None, 2, nh, dh, lw), lambda i: (i, 0, 0, 0, 0))
    per_q = pl.BlockSpec((None, nh, t_new, dh), lambda i: (i, 0, 0, 0))
    o, l, nc = pl.pallas_call(
        functools.partial(_cache_attn_kernel, window // dilation, dilation),
        grid=(b,),
        in_specs=[win, per_q, pl.BlockSpec((None, 2, nh, dh, t_new), lambda i: (i, 0, 0, 0, 0))],
        out_specs=[per_q, per_q, win],
        out_shape=[SDS((b, nh, t_new, dh), F32), SDS((b, nh, t_new, dh), F32), SDS((b, 2, nh, dh, lw), F32)],
        compiler_params=_cp("parallel"),
        name=f"cache_attn_d{dilation}",
    )(ct, qh, kvt)
    tok = lambda a: jnp.transpose(a, (0, 2, 1, 3)).reshape(b * t_new, nh * dh)
    return tok(o), tok(l), jnp.transpose(nc, (0, 4, 1, 2, 3))


def _log_sigmoid(x):
    return jnp.minimum(x, 0.0) - jnp.log1p(jnp.exp(-jnp.abs(x)))


def _mlstm_kernel(n_valid, q_ref, k_ref, v_ref, o_ref, g_ref, bif_ref, ng_ref, c0_ref, n0_ref, m0_ref,
                  h_ref, c1_ref, n1_ref, m1_ref, c_sc, n_sc, m_sc):
    ci = pl.program_id(1)
    nh, dk, dv = MLSTM_HEADS, MLSTM_DK, MLSTM_DV
    L = MLSTM_CHUNK

    @pl.when(ci == 0)
    def _():
        c_sc[...] = jnp.zeros(c_sc.shape, F32)
        n_sc[...] = jnp.zeros(n_sc.shape, F32)
        m_sc[...] = jnp.zeros(m_sc.shape, F32)
        c_sc[:, 0:dk, :] = c0_ref[...]
        n_sc[:, :, 0:dk] = n0_ref[...]
        m_sc[...] = m0_ref[...]

    def rows(ref):
        x = ref[...].astype(F32)
        if n_valid < L:
            x = jnp.concatenate([x, jnp.zeros((L - n_valid, x.shape[1]), F32)], axis=0)
        return x

    q = rows(q_ref).astype(BF16)
    k = rows(k_ref)
    v = rows(v_ref).astype(BF16)
    og = rows(o_ref)
    graw = rows(g_ref) + bif_ref[...]
    lane = lax.broadcasted_iota(I32, (L, LANES), 1)
    row = lax.broadcasted_iota(I32, (L, LANES), 0)
    is_i = lane < nh
    is_f = (lane >= nh) & (lane < 2 * nh)
    live = row < n_valid
    gi = jnp.where(is_i, jnp.where(live, graw, NEG_INF), 0.0)
    gf = jnp.where(is_f & live, _log_sigmoid(graw), 0.0)
    tril = (lax.broadcasted_iota(I32, (L, L), 0) >= lax.broadcasted_iota(I32, (L, L), 1))
    bcol = jnp.dot(tril.astype(F32), gf, precision=HIGHEST, preferred_element_type=F32)
    brow = bcol.T
    irow = gi.T
    scale = dk ** -0.5

    def heads(x):
        return jnp.stack([x[:, h * LANES:(h + 1) * LANES] for h in range(nh)], axis=0)

    def bdot(a, b, ca, cb):
        return lax.dot_general(a, b, (((ca,), (cb,)), ((0,), (0,))), preferred_element_type=F32)

    q3, v3 = heads(q), heads(v)
    k3 = heads(k)
    k3b = k3.astype(BF16)
    b_c = jnp.stack([bcol[:, nh + h:nh + h + 1] for h in range(nh)], axis=0)
    b_r = jnp.stack([brow[nh + h:nh + h + 1, :] for h in range(nh)], axis=0)
    i_r = jnp.stack([irow[h:h + 1, :] for h in range(nh)], axis=0)
    m_prev = m_sc[...]
    c3 = c_sc[...]
    n3 = n_sc[...]
    dlog = jnp.where(tril[None], b_c - b_r + i_r, NEG_INF)
    m_inter = b_c + m_prev
    m_t = jnp.maximum(m_inter, jnp.max(dlog, axis=-1, keepdims=True))
    sc = bdot(q3, k3b, 2, 2) * scale * jnp.exp(dlog - m_t)
    inter = jnp.exp(m_inter - m_t)
    num = bdot(sc.astype(BF16), v3, 2, 1) + inter * (bdot(q3, c3.astype(BF16), 2, 1) * scale)
    qn = jnp.sum(q3.astype(F32) * n3, axis=-1, keepdims=True) * scale
    den = jnp.sum(sc, axis=-1, keepdims=True) + inter * qn
    hh = num / jnp.maximum(jnp.abs(den), jnp.exp(-m_t))
    b_last = b_c[:, L - 1:L, :]
    g_r = b_last - b_r + i_r
    m_new = jnp.maximum(b_last + m_prev, jnp.max(g_r, axis=-1, keepdims=True))
    ws = jnp.exp(g_r - m_new)
    decay = jnp.exp(b_last + m_prev - m_new)
    kw = (jnp.stack([k3[h].T for h in range(nh)], axis=0) * ws).astype(BF16)
    c_sc[...] = decay * c3 + bdot(kw, v3, 2, 1)
    ws8 = jnp.broadcast_to(ws, (nh, 8, L)).astype(BF16)
    n_sc[...] = decay * n3 + bdot(ws8, k3b, 2, 1)[:, 0:1, :]
    m_sc[...] = m_new
    mu = jnp.mean(hh, axis=-1, keepdims=True)
    var = jnp.mean(jnp.square(hh - mu), axis=-1, keepdims=True)
    out = jax.nn.sigmoid(heads(og)) * ((hh - mu) * lax.rsqrt(var + EPS) * heads(ng_ref[...]))
    for h in range(nh):
        h_ref[:, h * LANES:(h + 1) * LANES] = out[h, 0:n_valid, :].astype(h_ref.dtype)

    @pl.when(ci == pl.num_programs(1) - 1)
    def _():
        c1_ref[...] = c_sc[:, 0:dk, :]
        n1_ref[...] = n_sc[:, :, 0:dk]
        m1_ref[...] = m_sc[...]


def _mlstm(qm, km, vm, om, gates, bif_pad, norm_g, c0, n0, m0, n_valid, out_dtype, name):
    b, s, w = qm.shape
    nh, dk, dv = MLSTM_HEADS, MLSTM_DK, MLSTM_DV
    nc = s // n_valid
    assert s % n_valid == 0 and (n_valid == MLSTM_CHUNK or nc == 1)
    tok = lambda width: pl.BlockSpec((None, n_valid, width), lambda i, c: (i, c, 0))
    const = lambda shape: pl.BlockSpec(shape, lambda i, c: (0,) * len(shape))
    h, c1, n1, m1 = pl.pallas_call(
        functools.partial(_mlstm_kernel, n_valid),
        grid=(b, nc),
        in_specs=[tok(w), tok(w), tok(w), tok(w), tok(LANES), const((1, LANES)), const((1, w)),
                  pl.BlockSpec((None, nh, dk, dv), lambda i, c: (i, 0, 0, 0)),
                  pl.BlockSpec((None, nh, 1, dk), lambda i, c: (i, 0, 0, 0)),
                  pl.BlockSpec((None, nh, 1, 1), lambda i, c: (i, 0, 0, 0))],
        out_specs=[tok(w),
                   pl.BlockSpec((None, nh, dk, dv), lambda i, c: (i, 0, 0, 0)),
                   pl.BlockSpec((None, nh, 1, dk), lambda i, c: (i, 0, 0, 0)),
                   pl.BlockSpec((None, nh, 1, 1), lambda i, c: (i, 0, 0, 0))],
        out_shape=[SDS((b, s, w), out_dtype), SDS((b, nh, dk, dv), F32), SDS((b, nh, 1, dk), F32),
                   SDS((b, nh, 1, 1), F32)],
        scratch_shapes=[pltpu.VMEM((nh, DK_PAD, dv), F32), pltpu.VMEM((nh, 1, DK_PAD), F32),
                        pltpu.VMEM((nh, 1, 1), F32)],
        compiler_params=_cp("parallel", "arbitrary"),
        name=name,
    )(qm, km, vm, om, gates, bif_pad, norm_g, c0, n0.reshape(b, nh, 1, dk), m0.reshape(b, nh, 1, 1))
    return h, c1, n1.reshape(b, nh, dk), m1.reshape(b, nh)


def _pack_bf16_pairs(x):
    w = x.shape[1] // 2
    bits = lax.bitcast_convert_type(x.astype(BF16).astype(F32), U32)
    return (bits[:, :w] >> 16) | (bits[:, w:] & jnp.uint32(0xFFFF0000))


def _unpack_bf16_pairs(p):
    lo = lax.bitcast_convert_type(p << 16, F32).astype(BF16)
    hi = lax.bitcast_convert_type(p & jnp.uint32(0xFFFF0000), F32).astype(BF16)
    return lo, hi


def _merge_kernel(x_ref, sc1_ref, sh1_ref, g1_ref, sc2_ref, sh2_ref,
                  o0_ref, o1_ref, o2_ref, l0_ref, l1_ref, l2_ref, mo_ref,
                  wg_ref, wpa_ref, wpm_ref, wout_ref, lng_ref, lnb_ref, x1_ref, h2p_ref, tok_sc):
    x = x_ref[...]
    d = x.shape[1]
    h = (x * (1.0 + sc1_ref[0]) + sh1_ref[0]).astype(BF16)
    g = jnp.dot(h, wg_ref[...], preferred_element_type=F32)

    def tokens(ref):
        dil, rows, _ = ref.shape
        if dil == 1:
            return ref[0]
        c = tok_sc.shape[0]
        for r in range(dil):
            for j in range(c):
                tok_sc[j, pl.ds(r, rows, stride=dil), :] = ref[r, :, j * LANES:(j + 1) * LANES]
        return jnp.concatenate([tok_sc[j] for j in range(c)], axis=1)

    l0, l1, l2 = tokens(l0_ref), tokens(l1_ref), tokens(l2_ref)
    lm = jnp.maximum(jnp.maximum(l0, l1), l2)
    e0, e1, e2 = jnp.exp(l0 - lm), jnp.exp(l1 - lm), jnp.exp(l2 - lm)
    ao = (e0 * tokens(o0_ref) + e1 * tokens(o1_ref) + e2 * tokens(o2_ref)) / (e0 + e1 + e2)
    pa = jnp.dot(ao.astype(BF16), wpa_ref[...], preferred_element_type=F32)
    pm = jnp.dot(mo_ref[...].astype(BF16), wpm_ref[...], preferred_element_type=F32)
    merged = jax.nn.sigmoid(g[:, :d]) * pa + jax.nn.sigmoid(g[:, d:]) * pm
    y = jnp.dot(merged.astype(BF16), wout_ref[...], preferred_element_type=F32)
    x1 = _layer_norm(ALPHA * x + g1_ref[0] * y, lng_ref[...], lnb_ref[...])
    x1_ref[...] = x1
    h2p_ref[...] = _pack_bf16_pairs(x1 * (1.0 + sc2_ref[0]) + sh2_ref[0])


def _merge(x, mods, attn, mo, wg, wpa, wpm, wout, lng, lnb, tm, tiles_per_batch, name):
    n, d = x.shape
    tok = lambda width: pl.BlockSpec((tm, width), lambda i: (i, 0))
    const = lambda a: pl.BlockSpec(a.shape, lambda i: (0,) * a.ndim)

    def lattice(a):
        dil, lat = a.shape[1], a.shape[2]
        per_batch = dil * lat // tm
        return pl.BlockSpec((None, dil, tm // dil, a.shape[3]), lambda i: (i // per_batch, 0, i % per_batch, 0))

    return pl.pallas_call(
        _merge_kernel,
        grid=(n // tm,),
        in_specs=[tok(d)] + [_mod_spec(m, tm, tiles_per_batch) for m in mods]
        + [lattice(a) for a in attn] + [tok(mo.shape[1])]
        + [const(a) for a in (wg, wpa, wpm, wout, lng, lnb)],
        out_specs=[tok(d), tok(d // 2)],
        out_shape=[SDS((n, d), F32), SDS((n, d // 2), U32)],
        scratch_shapes=[pltpu.VMEM((GROUP_WIDTH // LANES, tm, LANES), F32)],
        compiler_params=_cp("parallel"),
        name=name,
    )(x, *mods, *attn, mo, wg, wpa, wpm, wout, lng, lnb)


def _first_index_of_max(x, idx, big):
    mx = jnp.max(x, axis=0, keepdims=True)
    first = jnp.min(jnp.where(x == mx, idx, big), axis=0, keepdims=True)
    return mx, first


def _route_kernel(hp_ref, wlo_ref, whi_ref, bias_ref, idx_ref, w_ref, rank_ref, cnt_ref):
    tm = hp_ref.shape[0]
    ne = N_EXPERTS
    per = ne // N_EXPERT_GROUPS
    lo, hi = _unpack_bf16_pairs(hp_ref[...])
    logits = _nt_dot(wlo_ref[...], lo) + _nt_dot(whi_ref[...], hi)
    scores = jax.nn.sigmoid(logits)
    sel = scores + bias_ref[...]
    erow = lax.broadcasted_iota(I32, (ne, tm), 0).astype(F32)
    prow = lax.broadcasted_iota(I32, (per, tm), 0).astype(F32)
    gs = []
    for g in range(N_EXPERT_GROUPS):
        xg = sel[g * per:(g + 1) * per, :]
        m1, i1 = _first_index_of_max(xg, prow, per)
        m2 = jnp.max(jnp.where(prow == i1, NEG_INF, xg), axis=0, keepdims=True)
        gs.append(m1 + m2)
    gs = jnp.concatenate(gs, axis=0)
    grow = lax.broadcasted_iota(I32, gs.shape, 0).astype(F32)
    gkeep = jnp.zeros(gs.shape, F32)
    for _ in range(TOPK_GROUPS):
        _, gi = _first_index_of_max(gs, grow, N_EXPERT_GROUPS)
        pick = grow == gi
        gkeep = jnp.where(pick, 1.0, gkeep)
        gs = jnp.where(pick, NEG_INF, gs)
    keep = jnp.concatenate([jnp.broadcast_to(gkeep[g:g + 1, :], (per, tm)) for g in range(N_EXPERT_GROUPS)], axis=0)
    cand = jnp.where(keep > 0, sel, NEG_INF)
    member = jnp.zeros((ne, tm), F32)
    picks, idxs, ws = [], [], []
    for _ in range(TOP_K):
        _, ei = _first_index_of_max(cand, erow, ne)
        pick = erow == ei
        picks.append(pick)
        idxs.append(ei)
        ws.append(jnp.sum(jnp.where(pick, scores, 0.0), axis=0, keepdims=True))
        cand = jnp.where(pick, NEG_INF, cand)
        member = jnp.where(pick, 1.0, member)
    wsum = ws[0]
    for wk in ws[1:]:
        wsum = wsum + wk
    idx_ref[...] = jnp.concatenate(idxs, axis=0).astype(I32)
    w_ref[...] = jnp.concatenate(ws, axis=0) / wsum * ROUTED_SCALE
    ti = lax.broadcasted_iota(I32, (tm, tm), 0)
    tj = lax.broadcasted_iota(I32, (tm, tm), 1)
    before = (ti < tj).astype(BF16)
    mb = member.astype(BF16)
    prefix = jnp.dot(mb, before, preferred_element_type=F32)
    rank_ref[...] = jnp.concatenate(
        [jnp.sum(jnp.where(p, prefix, 0.0), axis=0, keepdims=True) for p in picks], axis=0).astype(I32)
    cnt_ref[...] = _nt_dot(jnp.ones((8, tm), BF16), mb).astype(I32)


def _route(h2p, w_router_t, bias_col):
    n, half = h2p.shape
    tm = ROUTE_TILE
    nt = n // tm
    wlo = w_router_t[:, :half]
    whi = w_router_t[:, half:]
    pair = lambda dt: SDS((TOP_K, n), dt)
    return pl.pallas_call(
        _route_kernel,
        grid=(nt,),
        in_specs=[pl.BlockSpec((tm, half), lambda i: (i, 0)),
                  pl.BlockSpec(wlo.shape, lambda i: (0, 0)),
                  pl.BlockSpec(whi.shape, lambda i: (0, 0)),
                  pl.BlockSpec(bias_col.shape, lambda i: (0, 0))],
        out_specs=[pl.BlockSpec((TOP_K, tm), lambda i: (0, i))] * 3
        + [pl.BlockSpec((None, 8, N_EXPERTS), lambda i: (i, 0, 0))],
        out_shape=[pair(I32), pair(F32), pair(I32), SDS((nt, 8, N_EXPERTS), I32)],
        compiler_params=_cp("parallel"),
        name="route",
    )(h2p, wlo, whi, bias_col)


def _dest_kernel(idx_ref, rank_ref, base_ref, dest_ref):
    tm = idx_ref.shape[1]
    erow = lax.broadcasted_iota(I32, (N_EXPERTS, tm), 0)
    base = base_ref[...]
    idx = idx_ref[...]
    rows = [jnp.sum(jnp.where(erow == idx[k:k + 1, :], base, 0.0), axis=0, keepdims=True) for k in range(TOP_K)]
    dest_ref[...] = jnp.concatenate(rows, axis=0).astype(I32) + rank_ref[...]


def _dest(idx, rank, base_cols, tile):
    n = idx.shape[1]
    per_route = ROUTE_TILE // tile
    return pl.pallas_call(
        _dest_kernel,
        grid=(n // tile,),
        in_specs=[pl.BlockSpec((TOP_K, tile), lambda i: (0, i)),
                  pl.BlockSpec((TOP_K, tile), lambda i: (0, i)),
                  pl.BlockSpec((None, N_EXPERTS, 1), lambda i: (i // per_route, 0, 0))],
        out_specs=pl.BlockSpec((None, TOP_K, tile), lambda i: (i, 0, 0)),
        out_shape=SDS((n // tile, TOP_K, tile), I32),
        compiler_params=_cp("parallel"),
        name="dest",
    )(idx, rank, base_cols)


def _row_copy_loop(tile, make_copy, start):
    def body(t, carry):
        for k in range(TOP_K):
            cp = make_copy(k, t)
            if start:
                cp.start(priority=k % 2)
            else:
                cp.wait()
        return carry
    lax.fori_loop(0, tile, body, 0, unroll=8)


def _dest_fetch(dest_hbm, tile_idx, dest_smem, sem_idx):
    n_idx = dest_smem.shape[0]
    start = tile_idx * n_idx
    if not isinstance(start, int):
        start = pl.multiple_of(start, n_idx)
    return pltpu.make_async_copy(dest_hbm.at[pl.ds(start, n_idx)], dest_smem, sem_idx)


def _rows_to_pieces(x, ref):
    c = x.shape[1] // LANES
    for j in range(c):
        ref[pl.ds(j, x.shape[0], stride=c), :] = x[:, j * LANES:(j + 1) * LANES]


def _pieces_to_rows(ref, rows, c):
    return jnp.concatenate([ref[pl.ds(j, rows, stride=c), :] for j in range(c)], axis=1)


def _piece(ref, row, c):
    return ref.at[pl.ds(pl.multiple_of(row * c, c), c)]


def _scatter_kernel(dest_hbm, hp_ref, xs_in, xs_out, stage, dest_smem, sem_idx, sem_rows):
    del xs_in
    i = pl.program_id(0)
    n = pl.num_programs(0)
    tile = hp_ref.shape[0]
    c = hp_ref.shape[1] // LANES
    slot = lax.rem(i, 2)
    idx_cp = _dest_fetch(dest_hbm, i, dest_smem, sem_idx)
    idx_cp.start()

    def row_copy(k, t):
        return pltpu.make_async_copy(_piece(stage.at[slot], t, c), _piece(xs_out, dest_smem[k * tile + t], c),
                                     sem_rows.at[slot])

    def wait_tile(s):
        _row_copy_loop(tile, lambda k, t: pltpu.make_async_copy(
            stage.at[s, pl.ds(0, c)], xs_out.at[pl.ds(0, c)], sem_rows.at[s]), False)

    @pl.when(i >= 2)
    def _():
        wait_tile(slot)

    _rows_to_pieces(hp_ref[...], stage.at[slot])
    idx_cp.wait()
    _row_copy_loop(tile, row_copy, True)

    @pl.when(i == n - 1)
    def _():
        @pl.when(n >= 2)
        def _():
            wait_tile(1 - slot)
        wait_tile(slot)


def _scatter_rows(dest_flat, h2p, n_rows):
    n, half = h2p.shape
    tile = MOE_TILE
    c = half // LANES
    xs0 = jnp.zeros((n_rows * c, LANES), U32)
    return pl.pallas_call(
        _scatter_kernel,
        grid=(n // tile,),
        in_specs=[pl.BlockSpec(memory_space=pl.ANY),
                  pl.BlockSpec((tile, half), lambda i: (i, 0)),
                  pl.BlockSpec(memory_space=pl.ANY)],
        out_specs=pl.BlockSpec(memory_space=pl.ANY),
        out_shape=SDS((n_rows * c, LANES), U32),
        scratch_shapes=[pltpu.VMEM((2, tile * c, LANES), U32), pltpu.SMEM((TOP_K * tile,), I32),
                        pltpu.SemaphoreType.DMA, pltpu.SemaphoreType.DMA((2,))],
        input_output_aliases={2: 0},
        compiler_params=_cp("arbitrary"),
        name="scatter_rows",
    )(dest_flat, h2p, xs0)


def _expert_kernel(blk_exp_ref, n_used_ref, xs_ref, win_ref, wout_ref, ys_ref, win_sc, wout_sc):
    i = pl.program_id(0)
    prev = blk_exp_ref[jnp.maximum(i - 1, 0)]

    @pl.when((i == 0) | (blk_exp_ref[i] != prev))
    def _():
        win_sc[...] = win_ref[...].astype(BF16)
        wout_sc[...] = wout_ref[...].astype(BF16)

    @pl.when(i < n_used_ref[0])
    def _():
        half = win_sc.shape[0] // 2
        lo, hi = _unpack_bf16_pairs(_pieces_to_rows(xs_ref, MOE_BLOCK, half // LANES))
        ag = (jnp.dot(lo, win_sc[0:half, :], preferred_element_type=F32)
              + jnp.dot(hi, win_sc[half:2 * half, :], preferred_element_type=F32))
        a = ag[:, :D_EXPERT]
        g = ag[:, D_EXPERT:]
        mid = (a * jax.nn.sigmoid(a) * g).astype(BF16)
        _rows_to_pieces(jnp.dot(mid, wout_sc[...], preferred_element_type=F32), ys_ref)

    @pl.when(i >= n_used_ref[0])
    def _():
        ys_ref[...] = jnp.zeros(ys_ref.shape, F32)


def _experts(xs, blk_exp, n_used, w_exp_in, w_exp_out):
    d = w_exp_in.shape[1]
    cx, cy = d // 2 // LANES, d // LANES
    n_rows = xs.shape[0] // cx
    nb = n_rows // MOE_BLOCK
    grid_spec = pltpu.PrefetchScalarGridSpec(
        num_scalar_prefetch=2,
        grid=(nb,),
        in_specs=[pl.BlockSpec((MOE_BLOCK * cx, LANES), lambda i, be, nu: (i, 0)),
                  pl.BlockSpec((None, d, 2 * D_EXPERT), lambda i, be, nu: (be[i], 0, 0)),
                  pl.BlockSpec((None, D_EXPERT, d), lambda i, be, nu: (be[i], 0, 0))],
        out_specs=pl.BlockSpec((MOE_BLOCK * cy, LANES), lambda i, be, nu: (i, 0)),
        scratch_shapes=[pltpu.VMEM((d, 2 * D_EXPERT), BF16), pltpu.VMEM((D_EXPERT, d), BF16)],
    )
    return pl.pallas_call(
        _expert_kernel,
        grid_spec=grid_spec,
        out_shape=SDS((n_rows * cy, LANES), F32),
        compiler_params=_cp("arbitrary"),
        name="experts",
    )(blk_exp, n_used, xs, w_exp_in, w_exp_out)


def _final_kernel(dest_hbm, x1_ref, hp_ref, w_ref, g2_ref, ys_hbm, wsin_ref, wsout_ref, lng_ref, lnb_ref,
                  y_ref, rows_sc, dest_smem, sem_idx, sem_rows, *, tile_offset):
    i = pl.program_id(0)
    n = pl.num_programs(0)
    tile = x1_ref.shape[0]
    c = x1_ref.shape[1] // LANES
    slot = lax.rem(i, 2)

    def start_rows(s):
        _row_copy_loop(tile, lambda k, t: pltpu.make_async_copy(
            _piece(ys_hbm, dest_smem[k * tile + t], c), _piece(rows_sc.at[s, k], t, c), sem_rows.at[s]), True)

    @pl.when(i == 0)
    def _():
        first = _dest_fetch(dest_hbm, tile_offset, dest_smem, sem_idx)
        first.start()
        first.wait()
        start_rows(0)

    nxt = _dest_fetch(dest_hbm, jnp.minimum(i + 1, n - 1) + tile_offset, dest_smem, sem_idx)

    @pl.when(i + 1 < n)
    def _():
        nxt.start()

    half = hp_ref.shape[1]
    lo, hi = _unpack_bf16_pairs(hp_ref[...])
    ag = (jnp.dot(lo, wsin_ref[0:half, :], preferred_element_type=F32)
          + jnp.dot(hi, wsin_ref[half:2 * half, :], preferred_element_type=F32))
    a = ag[:, :D_SHARED]
    g = ag[:, D_SHARED:]
    y2 = jnp.dot((a * jax.nn.sigmoid(a) * g).astype(BF16), wsout_ref[...], preferred_element_type=F32)
    w_cols = jnp.concatenate([w_ref[...], jnp.zeros((tile - TOP_K, tile), F32)], axis=0).T

    @pl.when(i + 1 < n)
    def _():
        nxt.wait()
        start_rows(1 - slot)

    _row_copy_loop(tile, lambda k, t: pltpu.make_async_copy(
        ys_hbm.at[pl.ds(0, c)], rows_sc.at[slot, 0, pl.ds(0, c)], sem_rows.at[slot]), False)
    for k in range(TOP_K):
        y2 = y2 + w_cols[:, k:k + 1] * _pieces_to_rows(rows_sc.at[slot, k], tile, c)
    y_ref[...] = _layer_norm(ALPHA * x1_ref[...] + g2_ref[0] * y2, lng_ref[...], lnb_ref[...])


def _final(dest_flat, x1, h2p, w_top, g2_3, ys, wsin, wsout, lng, lnb, tile_offset, tiles_per_batch, name):
    n, d = x1.shape
    tile = MOE_TILE
    assert tile == LANES
    const = lambda a: pl.BlockSpec(a.shape, lambda i: (0,) * a.ndim)
    return pl.pallas_call(
        functools.partial(_final_kernel, tile_offset=tile_offset),
        grid=(n // tile,),
        in_specs=[pl.BlockSpec(memory_space=pl.ANY),
                  pl.BlockSpec((tile, d), lambda i: (i, 0)),
                  pl.BlockSpec((tile, d // 2), lambda i: (i, 0)),
                  pl.BlockSpec((TOP_K, tile), lambda i: (0, i)),
                  _mod_spec(g2_3, tile, tiles_per_batch),
                  pl.BlockSpec(memory_space=pl.ANY),
                  const(wsin), const(wsout), const(lng), const(lnb)],
        out_specs=pl.BlockSpec((tile, d), lambda i: (i, 0)),
        out_shape=SDS((n, d), F32),
        scratch_shapes=[pltpu.VMEM((2, TOP_K, tile * (d // LANES), LANES), F32), pltpu.SMEM((TOP_K * tile,), I32),
                        pltpu.SemaphoreType.DMA, pltpu.SemaphoreType.DMA((2,))],
        compiler_params=_cp("arbitrary"),
        name=name,
    )(dest_flat, x1, h2p, w_top, g2_3, ys, wsin, wsout, lng, lnb)


def _sorted_layout(counts):
    totals = jnp.sum(counts, axis=0)
    padded = (totals + MOE_BLOCK - 1) // MOE_BLOCK * MOE_BLOCK
    pends = jnp.cumsum(padded)
    base = (pends - padded)[None, :] + jnp.cumsum(counts, axis=0) - counts
    return base.astype(I32), pends


def _moe_routed(h2p, w_router, router_bias, w_exp_in, w_exp_out):
    n = h2p.shape[0]
    idx, w_top, rank, cnt = _route(h2p, w_router.T.astype(BF16), router_bias.reshape(N_EXPERTS, 1))
    base, pends = _sorted_layout(cnt[:, 0, :])
    n_blocks = -(-(n * TOP_K + N_EXPERTS * (MOE_BLOCK - 1)) // MOE_BLOCK)
    blk_start = jnp.arange(n_blocks, dtype=I32) * MOE_BLOCK
    blk_exp = jnp.minimum(jnp.sum((pends[None, :] <= blk_start[:, None]).astype(I32), axis=1), N_EXPERTS - 1)
    n_used = (pends[-1:] // MOE_BLOCK).astype(I32)
    dest = _dest(idx, rank, base.astype(F32)[:, :, None], MOE_TILE).reshape(-1)
    xs = _scatter_rows(dest, h2p, n_blocks * MOE_BLOCK)
    ys = _experts(xs, blk_exp, n_used, w_exp_in, w_exp_out)
    return dest, w_top, ys


def _pad_heads(w, nh, dk):
    d = w.shape[0]
    return jnp.pad(w.reshape(d, nh, dk), ((0, 0), (0, 0), (0, DK_PAD - dk))).reshape(d, nh * DK_PAD)


def _split_w_in(w_in):
    aw = N_GROUPS * GROUP_WIDTH
    nh, dk, dv = MLSTM_HEADS, MLSTM_DK, MLSTM_DV
    offs = np.cumsum([0, aw, aw, aw, nh * dk, nh * dk, nh * dv, nh * dv, nh, nh, w_in.shape[0], w_in.shape[0]])
    seg = [w_in[:, offs[i]:offs[i + 1]] for i in range(11)]
    q_a, k_a, v_a, q_m, k_m, v_m, o_m, i_m, f_m, g_a, g_b = seg
    bf = lambda a: a.astype(BF16)
    w_gates = jnp.pad(jnp.concatenate([i_m, f_m], axis=1), ((0, 0), (0, LANES - 2 * nh)))
    return dict(qa=bf(q_a), kva=bf(jnp.concatenate([k_a, v_a], axis=1)),
                qm=bf(_pad_heads(q_m, nh, dk)), km=bf(_pad_heads(k_m, nh, dk)), vm=bf(v_m), om=bf(o_m),
                gates=w_gates, g=bf(jnp.concatenate([g_a, g_b], axis=1)))


def _mixing(x, mods, tm, tiles_per_batch, batch, seq, caches, states, wts, prompt):
    n, d = x.shape
    sc1, sh1, g1 = mods["scale1"], mods["shift1"], mods["gate1"]
    sc2, sh2 = mods["scale2"], mods["shift2"]
    act = BF16 if prompt else F32
    if prompt:
        tail = min(max(w for w, _ in ATTN_GROUPS), seq)
        qkv_l, kv_tail = _proj_lattice(x, sc1, sh1, wts["qa"], wts["kva"], batch, seq, tm, tail)
    else:
        q_a, kv_a = _proj(x, sc1, sh1, [wts["qa"], wts["kva"]], [F32, F32], None, tm, tiles_per_batch, "proj_attn_s")
    qm, km, vm, om, gates = _proj(x, sc1, sh1, [wts["qm"], wts["km"], wts["vm"], wts["om"]],
                                  [act, act, act, F32], wts["gates"], tm, tiles_per_batch,
                                  "proj_mlstm_p" if prompt else "proj_mlstm_s")
    attn_o, attn_l, bufs = [], [], []
    for g, (window, dilation) in enumerate(ATTN_GROUPS):
        if prompt:
            o, l = _band_attn(*qkv_l[g], window)
            keep = min(window, seq)
            kv3 = kv_tail.reshape(batch, tail, 2, N_GROUPS, GROUP_WIDTH)[:, tail - keep:, :, g, :]
            bufs.append(kv3.reshape(1, batch, keep, 2, ATTN_HEADS, ATTN_HEAD_DIM))
        else:
            o, l, nc = _cache_attn(q_a.reshape(batch, seq, -1), kv_a.reshape(batch, seq, -1), caches[g], g,
                                   window, dilation)
            o, l = o[None, None], l[None, None]
            bufs.append(nc[None])
        attn_o.append(o)
        attn_l.append(l)
    r3 = lambda a: a.reshape(batch, seq, a.shape[-1])
    n_valid = MLSTM_CHUNK if prompt else seq
    mo, c1, n1, m1 = _mlstm(r3(qm), r3(km), r3(vm), r3(om), r3(gates), wts["bif"], wts["norm_g"],
                            states[0], states[1], states[2], n_valid, act,
                            "mlstm_p" if prompt else "mlstm_s")
    x1, h2p = _merge(x, [sc1, sh1, g1, sc2, sh2], attn_o + attn_l, mo.reshape(n, -1),
                     wts["g"], wts["pa"], wts["pm"], wts["out"], wts["ln1_g"], wts["ln1_b"],
                     tm, tiles_per_batch, "merge_p" if prompt else "merge_s")
    return x1, h2p, bufs, (c1[None], n1[None], m1[None])


def _mod_pieces(mod, d, rows_per_batch, tm):
    names = ("shift1", "scale1", "gate1", "shift2", "scale2", "gate2")
    out = {}
    for p, name in enumerate(names):
        piece = mod[:, p * d:(p + 1) * d]
        if rows_per_batch % tm == 0:
            out[name] = piece[:, None, :]
        else:
            out[name] = jnp.repeat(piece, rows_per_batch, axis=0).reshape(-1, tm, d)
    return out


def kernel(x_prompt, x_sample, cache_kv_w128, cache_kv_w512, cache_kv_w2048, state_mlstm_C, state_mlstm_n, state_mlstm_m, c_prompt, c_sample, w_ada, b_ada, w_in, b_if, mlstm_norm_g, w_proj_attn, w_proj_mlstm, w_out, ln1_g, ln1_b, w_router, router_bias, w_exp_in, w_exp_out, w_sh_in, w_sh_out, ln2_g, ln2_b):
    assert w_ada.shape[0] == DEPTH
    bp, sp, d = x_prompt.shape
    bs, ss, _ = x_sample.shape
    nh = MLSTM_HEADS
    np_, ns = bp * sp, bs * ss

    mod = _ada(jnp.concatenate([c_prompt, c_sample], axis=0), w_ada[0], b_ada[0])
    wts = _split_w_in(w_in[0])
    wts.update(
        bif=jnp.pad(b_if[0], (0, LANES - 2 * nh)).reshape(1, LANES),
        norm_g=mlstm_norm_g[0].reshape(1, -1),
        pa=w_proj_attn[0].astype(BF16), pm=w_proj_mlstm[0].astype(BF16), out=w_out[0].astype(BF16),
        ln1_g=ln1_g[0].reshape(1, d), ln1_b=ln1_b[0].reshape(1, d))

    tm_p, tm_s = 512, 256
    mods_p = _mod_pieces(mod[:bp], d, sp, tm_p)
    mods_s = _mod_pieces(mod[bp:], d, ss, tm_s)
    zeros_p = (jnp.zeros((bp, nh, MLSTM_DK, MLSTM_DV), F32), jnp.zeros((bp, nh, MLSTM_DK), F32),
               jnp.zeros((bp, nh), F32))
    x1p, h2p_p, bufs_p, st_p = _mixing(x_prompt.reshape(np_, d), mods_p, tm_p, sp // tm_p, bp, sp, None,
                                       zeros_p, wts, True)
    caches = (cache_kv_w128[0], cache_kv_w512[0], cache_kv_w2048[0])
    states = (state_mlstm_C[0], state_mlstm_n[0], state_mlstm_m[0])
    x1s, h2p_s, bufs_s, st_s = _mixing(x_sample.reshape(ns, d), mods_s, tm_s, 1, bs, ss, caches, states, wts, False)

    h2p = jnp.concatenate([h2p_p, h2p_s], axis=0)
    dest, w_top, ys = _moe_routed(h2p, w_router[0], router_bias[0], w_exp_in[0], w_exp_out[0])
    wsin, wsout = w_sh_in[0].astype(BF16), w_sh_out[0].astype(BF16)
    lng, lnb = ln2_g[0].reshape(1, d), ln2_b[0].reshape(1, d)
    g2_p = _mod_pieces(mod[:bp], d, sp, MOE_TILE)["gate2"]
    g2_s = _mod_pieces(mod[bp:], d, ss, MOE_TILE)["gate2"]
    y_p = _final(dest, x1p, h2p_p, w_top[:, :np_], g2_p, ys, wsin, wsout, lng, lnb, 0, sp // MOE_TILE, "final_p")
    y_s = _final(dest, x1s, h2p_s, w_top[:, np_:], g2_s, ys, wsin, wsout, lng, lnb, np_ // MOE_TILE, 1, "final_s")

    return (y_p.reshape(bp, sp, d), y_s.reshape(bs, ss, d),
            bufs_p[0], bufs_p[1], bufs_p[2], st_p[0], st_p[1], st_p[2],
            bufs_s[0], bufs_s[1], bufs_s[2], st_s[0], st_s[1], st_s[2])
```

```python
import functools

import numpy as np
import jax
import jax.numpy as jnp
from jax import lax
from jax.experimental import pallas as pl
from jax.experimental.pallas import tpu as pltpu

F32 = jnp.float32
BF16 = jnp.bfloat16
I32 = jnp.int32
U32 = jnp.uint32
HIGHEST = lax.Precision.HIGHEST
SDS = jax.ShapeDtypeStruct
NEG_INF = float("-inf")

ATTN_GROUPS = ((128, 1), (512, 4), (2048, 16))
N_GROUPS = 3
ATTN_HEADS = 4
ATTN_HEAD_DIM = 64
GROUP_WIDTH = ATTN_HEADS * ATTN_HEAD_DIM
ATTN_BLOCK = 128
MLSTM_HEADS = 8
MLSTM_DK = 64
MLSTM_DV = 128
N_EXPERTS = 256
TOP_K = 8
N_EXPERT_GROUPS = 8
TOPK_GROUPS = 4
D_EXPERT = 256
D_SHARED = 256
ROUTED_SCALE = 2.5
DEPTH = 1
ALPHA = (2 * DEPTH) ** 0.25
EPS = 1e-5

LANES = 128
MLSTM_CHUNK = 128
DK_PAD = LANES
MOE_TILE = 128
MOE_BLOCK = 512
ROUTE_TILE = 512
VMEM_LIMIT = 56 << 20


def _cp(*sem, vmem=VMEM_LIMIT):
    return pltpu.CompilerParams(dimension_semantics=sem, vmem_limit_bytes=vmem)


def _nt_dot(a, b):
    return lax.dot_general(a, b, (((1,), (1,)), ((), ())), preferred_element_type=F32)


def _layer_norm(r, g, b):
    mu = jnp.mean(r, axis=-1, keepdims=True)
    var = jnp.mean(jnp.square(r - mu), axis=-1, keepdims=True)
    return (r - mu) * lax.rsqrt(var + EPS) * g + b


def _mod_spec(mod3, tm, tiles_per_batch):
    d = mod3.shape[-1]
    if mod3.shape[1] == 1:
        return pl.BlockSpec((1, 1, d), lambda i: (i // tiles_per_batch, 0, 0))
    return pl.BlockSpec((1, tm, d), lambda i: (i, 0, 0))


def _ada_kernel(c_ref, w_ref, b_ref, o_ref):
    c = c_ref[...]
    s = c * jax.nn.sigmoid(c)
    o_ref[...] = jnp.dot(s, w_ref[...], precision=HIGHEST, preferred_element_type=F32) + b_ref[...]


def _ada(c_all, w_ada, b_ada):
    r, d = c_all.shape
    n = w_ada.shape[1]
    tn = 1024
    return pl.pallas_call(
        _ada_kernel,
        grid=(n // tn,),
        in_specs=[pl.BlockSpec((r, d), lambda j: (0, 0)),
                  pl.BlockSpec((d, tn), lambda j: (0, j)),
                  pl.BlockSpec((1, tn), lambda j: (0, j))],
        out_specs=pl.BlockSpec((r, tn), lambda j: (0, j)),
        out_shape=SDS((r, n), F32),
        compiler_params=_cp("parallel"),
        name="ada",
    )(c_all, w_ada, b_ada.reshape(1, n))


def _proj_kernel(n_w, has_hp, x_ref, sc_ref, sh_ref, *refs):
    w_refs = refs[:n_w]
    pos = n_w
    if has_hp:
        whp_ref = refs[pos]
        pos += 1
    o_refs = refs[pos:pos + n_w]
    h = x_ref[...] * (1.0 + sc_ref[0]) + sh_ref[0]
    hb = h.astype(BF16)
    for w_ref, o_ref in zip(w_refs, o_refs):
        o_ref[...] = jnp.dot(hb, w_ref[...], preferred_element_type=F32).astype(o_ref.dtype)
    if has_hp:
        ohp_ref = refs[pos + n_w]
        ohp_ref[...] = jnp.dot(h, whp_ref[...], precision=HIGHEST, preferred_element_type=F32)


def _proj(x, sc3, sh3, weights, out_dtypes, w_hp, tm, tiles_per_batch, name):
    n, d = x.shape
    n_w = len(weights)
    has_hp = w_hp is not None
    in_specs = [pl.BlockSpec((tm, d), lambda i: (i, 0)),
                _mod_spec(sc3, tm, tiles_per_batch), _mod_spec(sh3, tm, tiles_per_batch)]
    in_specs += [pl.BlockSpec(w.shape, lambda i: (0, 0)) for w in weights]
    out_specs = [pl.BlockSpec((tm, w.shape[1]), lambda i: (i, 0)) for w in weights]
    out_shape = [SDS((n, w.shape[1]), dt) for w, dt in zip(weights, out_dtypes)]
    args = [x, sc3, sh3, *weights]
    if has_hp:
        in_specs.append(pl.BlockSpec(w_hp.shape, lambda i: (0, 0)))
        out_specs.append(pl.BlockSpec((tm, w_hp.shape[1]), lambda i: (i, 0)))
        out_shape.append(SDS((n, w_hp.shape[1]), F32))
        args.append(w_hp)
    return pl.pallas_call(
        functools.partial(_proj_kernel, n_w, has_hp),
        grid=(n // tm,),
        in_specs=in_specs, out_specs=out_specs, out_shape=out_shape,
        compiler_params=_cp("parallel"),
        name=name,
    )(*args)


def _head_masks(width):
    lane_head = lax.broadcasted_iota(I32, (1, width), 1) // ATTN_HEAD_DIM
    return [(lane_head == h).astype(F32) for h in range(ATTN_HEADS)]


def _band_attn_kernel(wband, q_ref, kc_ref, kp_ref, vc_ref, vp_ref, o_ref, l_ref):
    j = pl.program_id(2)
    qb = q_ref.shape[0]
    q = q_ref[...].astype(F32)
    k = jnp.concatenate([kp_ref[...], kc_ref[...]], axis=0).astype(BF16)
    v = jnp.concatenate([vp_ref[...], vc_ref[...]], axis=0).astype(BF16)
    qi = lax.broadcasted_iota(I32, (qb, 2 * qb), 0)
    kj = lax.broadcasted_iota(I32, (qb, 2 * qb), 1)
    dist = qi + qb - kj
    valid = jnp.where(dist >= 0, 1, 0) * jnp.where(dist <= wband, 1, 0) * jnp.where(j * qb + kj - qb >= 0, 1, 0)
    bias = jnp.where(valid > 0, 0.0, NEG_INF)
    hms = _head_masks(q.shape[1])
    qs = jnp.concatenate([q * (hm * ATTN_HEAD_DIM ** -0.5) for hm in hms], axis=0).astype(BF16)
    s = _nt_dot(qs, k) + jnp.concatenate([bias] * ATTN_HEADS, axis=0)
    mx = jnp.max(s, axis=-1, keepdims=True)
    p = jnp.exp(s - mx)
    den = jnp.sum(p, axis=-1, keepdims=True)
    o4 = jnp.dot((p / den).astype(BF16), v, preferred_element_type=F32)
    l4 = mx + jnp.log(den)
    o_acc = jnp.zeros(o_ref.shape, F32)
    l_acc = jnp.zeros(l_ref.shape, F32)
    for h, hm in enumerate(hms):
        o_acc = o_acc + o4[h * qb:(h + 1) * qb, :] * hm
        l_acc = l_acc + l4[h * qb:(h + 1) * qb, :] * hm
    o_ref[...] = o_acc
    l_ref[...] = l_acc


def _band_attn(q_l, k_l, v_l, window):
    batch, dilation, lat, gw = q_l.shape
    nb = lat // ATTN_BLOCK
    blk = (None, None, ATTN_BLOCK, gw)
    cur = lambda b, r, j: (b, r, j, 0)
    prev = lambda b, r, j: (b, r, jnp.maximum(j - 1, 0), 0)
    return pl.pallas_call(
        functools.partial(_band_attn_kernel, window // dilation),
        grid=(batch, dilation, nb),
        in_specs=[pl.BlockSpec(blk, cur), pl.BlockSpec(blk, cur), pl.BlockSpec(blk, prev),
                  pl.BlockSpec(blk, cur), pl.BlockSpec(blk, prev)],
        out_specs=[pl.BlockSpec(blk, cur), pl.BlockSpec(blk, cur)],
        out_shape=[SDS(q_l.shape, F32)] * 2,
        compiler_params=_cp("parallel", "parallel", "arbitrary"),
        name=f"band_attn_d{dilation}",
    )(q_l, k_l, k_l, v_l, v_l)


def _rows_to_lane_tiles(x, ref):
    for j in range(x.shape[1] // LANES):
        ref[j] = x[:, j * LANES:(j + 1) * LANES]


def _proj_lattice_kernel(dils, t0, x_ref, sc_ref, sh_ref, wq_ref, wkv_ref, *refs):
    ng = len(dils)
    lat_refs = refs[:3 * ng]
    kv_tail_ref = refs[3 * ng]
    zs = refs[3 * ng + 1]
    gw = GROUP_WIDTH
    tm = x_ref.shape[0]
    hb = (x_ref[...] * (1.0 + sc_ref[0]) + sh_ref[0]).astype(BF16)
    zq = jnp.dot(hb, wq_ref[...], preferred_element_type=F32)
    zkv = jnp.dot(hb, wkv_ref[...], preferred_element_type=F32)

    @pl.when(pl.program_id(1) >= t0)
    def _():
        kv_tail_ref[...] = zkv

    for g, d in enumerate(dils):
        for part, z in enumerate((zq[:, g * gw:(g + 1) * gw], zkv[:, g * gw:(g + 1) * gw],
                                  zkv[:, (ng + g) * gw:(ng + g + 1) * gw])):
            out = lat_refs[3 * g + part]
            if d == 1:
                out[0] = z.astype(out.dtype)
            else:
                _rows_to_lane_tiles(z, zs)
                for r in range(d):
                    out[r] = jnp.concatenate([zs[j, pl.ds(r, tm // d, stride=d), :] for j in range(gw // LANES)],
                                             axis=1).astype(out.dtype)


def _proj_lattice(x, sc3, sh3, wq, wkv, batch, seq, tm, tail):
    d_model = x.shape[1]
    tiles = seq // tm
    t0 = (seq - tail) // tm
    assert seq % tm == 0 and (seq - tail) % tm == 0
    dils = tuple(d for _, d in ATTN_GROUPS)
    gw = GROUP_WIDTH
    lat_specs, lat_shapes = [], []
    for d in dils:
        for _ in range(3):
            lat_specs.append(pl.BlockSpec((None, d, tm // d, gw), lambda b, t: (b, 0, t, 0)))
            lat_shapes.append(SDS((batch, d, seq // d, gw), BF16))
    outs = pl.pallas_call(
        functools.partial(_proj_lattice_kernel, dils, t0),
        grid=(batch, tiles),
        in_specs=[pl.BlockSpec((tm, d_model), lambda b, t: (b * tiles + t, 0)),
                  pl.BlockSpec((1, 1, d_model), lambda b, t: (b, 0, 0)),
                  pl.BlockSpec((1, 1, d_model), lambda b, t: (b, 0, 0)),
                  pl.BlockSpec(wq.shape, lambda b, t: (0, 0)),
                  pl.BlockSpec(wkv.shape, lambda b, t: (0, 0))],
        out_specs=lat_specs + [pl.BlockSpec((None, tm, wkv.shape[1]), lambda b, t: (b, jnp.maximum(t - t0, 0), 0))],
        out_shape=lat_shapes + [SDS((batch, tail, wkv.shape[1]), F32)],
        scratch_shapes=[pltpu.VMEM((gw // LANES, tm, LANES), F32)],
        compiler_params=_cp("parallel", "arbitrary"),
        name="proj_attn_p",
    )(x, sc3, sh3, wq, wkv)
    return [outs[3 * g:3 * g + 3] for g in range(len(dils))], outs[-1]


def _cache_attn_kernel(wband, dilation, c_ref, q_ref, kv_ref, o_ref, l_ref, nc_ref):
    lw = c_ref.shape[-1]
    t_new = q_ref.shape[1]
    dh = q_ref.shape[2]

    def band_bias(n_keys, key_base):
        t = lax.broadcasted_iota(I32, (t_new, n_keys), 0)
        p = lax.broadcasted_iota(I32, (t_new, n_keys), 1) + key_base
        delta = lw + t - p
        ok = (jnp.where(delta >= 0, 1, 0) * jnp.where((delta & (dilation - 1)) == 0, 1, 0)
              * jnp.where(delta <= wband * dilation, 1, 0))
        return jnp.where(ok > 0, 0.0, NEG_INF)

    bias_c = band_bias(lw, 0)
    bias_n = band_bias(t_new, lw)
    place = (lax.broadcasted_iota(I32, (t_new, LANES), 1)
             == lax.broadcasted_iota(I32, (t_new, LANES), 0) + (LANES - t_new)).astype(F32)
    tail = lax.broadcasted_iota(I32, (1, LANES), 1) >= LANES - t_new
    for h in range(ATTN_HEADS):
        qh = (q_ref[h] * dh ** -0.5).astype(BF16)
        kt, vt = c_ref[0, h], c_ref[1, h]
        knt, vnt = kv_ref[0, h], kv_ref[1, h]
        sc = jnp.dot(qh, kt.astype(BF16), preferred_element_type=F32) + bias_c
        sn = jnp.dot(qh, knt.astype(BF16), preferred_element_type=F32) + bias_n
        mx = jnp.maximum(jnp.max(sc, axis=-1, keepdims=True), jnp.max(sn, axis=-1, keepdims=True))
        pc = jnp.exp(sc - mx)
        pn = jnp.exp(sn - mx)
        den = jnp.sum(pc, axis=-1, keepdims=True) + jnp.sum(pn, axis=-1, keepdims=True)
        o_ref[h] = (_nt_dot((pc / den).astype(BF16), vt.astype(BF16))
                    + _nt_dot((pn / den).astype(BF16), vnt.astype(BF16)))
        l_ref[h] = jnp.broadcast_to(mx + jnp.log(den), (t_new, dh))
        for kv, (old, new) in enumerate(((kt, knt), (vt, vnt))):
            rolled = pltpu.roll(old, lw - t_new, axis=1)
            new_tile = jnp.dot(new, place, precision=HIGHEST, preferred_element_type=F32)
            if lw > LANES:
                nc_ref[kv, h, :, 0:lw - LANES] = rolled[:, 0:lw - LANES]
            nc_ref[kv, h, :, lw - LANES:lw] = jnp.where(tail, new_tile, rolled[:, lw - LANES:lw])


def _cache_attn(q_s, kv_s, cache, g, window, dilation):
    b, lw = cache.shape[0], cache.shape[1]
    t_new = q_s.shape[1]
    nh, dh = ATTN_HEADS, ATTN_HEAD_DIM
    assert lw == window and lw % LANES == 0 and t_new % 8 == 0 and dilation & (dilation - 1) == 0
    ct = jnp.transpose(cache, (0, 2, 3, 4, 1))
    qh = jnp.transpose(q_s.reshape(b, t_new, N_GROUPS, nh, dh)[:, :, g], (0, 2, 1, 3))
    kvt = jnp.transpose(kv_s.reshape(b, t_new, 2, N_GROUPS, nh, dh)[:, :, :, g], (0, 2, 3, 4, 1))
    win = pl.BlockSpec((None, 2, nh, dh, lw), lambda i: (i, 0, 0, 0, 0))
    per_q = pl.BlockSpec((None, nh, t_new, dh), lambda i: (i, 0, 0, 0))
    o, l, nc = pl.pallas_call(
        functools.partial(_cache_attn_kernel, window // dilation, dilation),
        grid=(b,),
        in_specs=[win, per_q, pl.BlockSpec((None, 2, nh, dh, t_new), lambda i: (i, 0, 0, 0, 0))],
        out_specs=[per_q, per_q, win],
        out_shape=[SDS((b, nh, t_new, dh), F32), SDS((b, nh, t_new, dh), F32), SDS((b, 2, nh, dh, lw), F32)],
        compiler_params=_cp("parallel"),
        name=f"cache_attn_d{dilation}",
    )(ct, qh, kvt)
    tok = lambda a: jnp.transpose(a, (0, 2, 1, 3)).reshape(b * t_new, nh * dh)
    return tok(o), tok(l), jnp.transpose(nc, (0, 4, 1, 2, 3))


def _log_sigmoid(x):
    return jnp.minimum(x, 0.0) - jnp.log1p(jnp.exp(-jnp.abs(x)))


def _mlstm_kernel(n_valid, q_ref, k_ref, v_ref, o_ref, g_ref, bif_ref, ng_ref, c0_ref, n0_ref, m0_ref,
                  h_ref, c1_ref, n1_ref, m1_ref, c_sc, n_sc, m_sc):
    ci = pl.program_id(1)
    nh, dk, dv = MLSTM_HEADS, MLSTM_DK, MLSTM_DV
    L = MLSTM_CHUNK

    @pl.when(ci == 0)
    def _():
        c_sc[...] = jnp.zeros(c_sc.shape, F32)
        n_sc[...] = jnp.zeros(n_sc.shape, F32)
        m_sc[...] = jnp.zeros(m_sc.shape, F32)
        c_sc[:, 0:dk, :] = c0_ref[...]
        n_sc[:, :, 0:dk] = n0_ref[...]
        m_sc[...] = m0_ref[...]

    def rows(ref):
        x = ref[...].astype(F32)
        if n_valid < L:
            x = jnp.concatenate([x, jnp.zeros((L - n_valid, x.shape[1]), F32)], axis=0)
        return x

    q = rows(q_ref).astype(BF16)
    k = rows(k_ref)
    v = rows(v_ref).astype(BF16)
    og = rows(o_ref)
    graw = rows(g_ref) + bif_ref[...]
    lane = lax.broadcasted_iota(I32, (L, LANES), 1)
    row = lax.broadcasted_iota(I32, (L, LANES), 0)
    is_i = lane < nh
    is_f = (lane >= nh) & (lane < 2 * nh)
    live = row < n_valid
    gi = jnp.where(is_i, jnp.where(live, graw, NEG_INF), 0.0)
    gf = jnp.where(is_f & live, _log_sigmoid(graw), 0.0)
    tril = (lax.broadcasted_iota(I32, (L, L), 0) >= lax.broadcasted_iota(I32, (L, L), 1))
    bcol = jnp.dot(tril.astype(F32), gf, precision=HIGHEST, preferred_element_type=F32)
    brow = bcol.T
    irow = gi.T
    scale = dk ** -0.5

    def heads(x):
        return jnp.stack([x[:, h * LANES:(h + 1) * LANES] for h in range(nh)], axis=0)

    def bdot(a, b, ca, cb):
        return lax.dot_general(a, b, (((ca,), (cb,)), ((0,), (0,))), preferred_element_type=F32)

    q3, v3 = heads(q), heads(v)
    k3 = heads(k)
    k3b = k3.astype(BF16)
    b_c = jnp.stack([bcol[:, nh + h:nh + h + 1] for h in range(nh)], axis=0)
    b_r = jnp.stack([brow[nh + h:nh + h + 1, :] for h in range(nh)], axis=0)
    i_r = jnp.stack([irow[h:h + 1, :] for h in range(nh)], axis=0)
    m_prev = m_sc[...]
    c3 = c_sc[...]
    n3 = n_sc[...]
    dlog = jnp.where(tril[None], b_c - b_r + i_r, NEG_INF)
    m_inter = b_c + m_prev
    m_t = jnp.maximum(m_inter, jnp.max(dlog, axis=-1, keepdims=True))
    sc = bdot(q3, k3b, 2, 2) * scale * jnp.exp(dlog - m_t)
    inter = jnp.exp(m_inter - m_t)
    num = bdot(sc.astype(BF16), v3, 2, 1) + inter * (bdot(q3, c3.astype(BF16), 2, 1) * scale)
    qn = jnp.sum(q3.astype(F32) * n3, axis=-1, keepdims=True) * scale
    den = jnp.sum(sc, axis=-1, keepdims=True) + inter * qn
    hh = num / jnp.maximum(jnp.abs(den), jnp.exp(-m_t))
    b_last = b_c[:, L - 1:L, :]
    g_r = b_last - b_r + i_r
    m_new = jnp.maximum(b_last + m_prev, jnp.max(g_r, axis=-1, keepdims=True))
    ws = jnp.exp(g_r - m_new)
    decay = jnp.exp(b_last + m_prev - m_new)
    kw = (jnp.stack([k3[h].T for h in range(nh)], axis=0) * ws).astype(BF16)
    c_sc[...] = decay * c3 + bdot(kw, v3, 2, 1)
    ws8 = jnp.broadcast_to(ws, (nh, 8, L)).astype(BF16)
    n_sc[...] = decay * n3 + bdot(ws8, k3b, 2, 1)[:, 0:1, :]
    m_sc[...] = m_new
    mu = jnp.mean(hh, axis=-1, keepdims=True)
    var = jnp.mean(jnp.square(hh - mu), axis=-1, keepdims=True)
    out = jax.nn.sigmoid(heads(og)) * ((hh - mu) * lax.rsqrt(var + EPS) * heads(ng_ref[...]))
    for h in range(nh):
        h_ref[:, h * LANES:(h + 1) * LANES] = out[h, 0:n_valid, :].astype(h_ref.dtype)

    @pl.when(ci == pl.num_programs(1) - 1)
    def _():
        c1_ref[...] = c_sc[:, 0:dk, :]
        n1_ref[...] = n_sc[:, :, 0:dk]
        m1_ref[...] = m_sc[...]


def _mlstm(qm, km, vm, om, gates, bif_pad, norm_g, c0, n0, m0, n_valid, out_dtype, name):
    b, s, w = qm.shape
    nh, dk, dv = MLSTM_HEADS, MLSTM_DK, MLSTM_DV
    nc = s // n_valid
    assert s % n_valid == 0 and (n_valid == MLSTM_CHUNK or nc == 1)
    tok = lambda width: pl.BlockSpec((None, n_valid, width), lambda i, c: (i, c, 0))
    const = lambda shape: pl.BlockSpec(shape, lambda i, c: (0,) * len(shape))
    h, c1, n1, m1 = pl.pallas_call(
        functools.partial(_mlstm_kernel, n_valid),
        grid=(b, nc),
        in_specs=[tok(w), tok(w), tok(w), tok(w), tok(LANES), const((1, LANES)), const((1, w)),
                  pl.BlockSpec((None, nh, dk, dv), lambda i, c: (i, 0, 0, 0)),
                  pl.BlockSpec((None, nh, 1, dk), lambda i, c: (i, 0, 0, 0)),
                  pl.BlockSpec((None, nh, 1, 1), lambda i, c: (i, 0, 0, 0))],
        out_specs=[tok(w),
                   pl.BlockSpec((None, nh, dk, dv), lambda i, c: (i, 0, 0, 0)),
                   pl.BlockSpec((None, nh, 1, dk), lambda i, c: (i, 0, 0, 0)),
                   pl.BlockSpec((None, nh, 1, 1), lambda i, c: (i, 0, 0, 0))],
        out_shape=[SDS((b, s, w), out_dtype), SDS((b, nh, dk, dv), F32), SDS((b, nh, 1, dk), F32),
                   SDS((b, nh, 1, 1), F32)],
        scratch_shapes=[pltpu.VMEM((nh, DK_PAD, dv), F32), pltpu.VMEM((nh, 1, DK_PAD), F32),
                        pltpu.VMEM((nh, 1, 1), F32)],
        compiler_params=_cp("parallel", "arbitrary"),
        name=name,
    )(qm, km, vm, om, gates, bif_pad, norm_g, c0, n0.reshape(b, nh, 1, dk), m0.reshape(b, nh, 1, 1))
    return h, c1, n1.reshape(b, nh, dk), m1.reshape(b, nh)


def _pack_bf16_pairs(x):
    w = x.shape[1] // 2
    bits = lax.bitcast_convert_type(x.astype(BF16).astype(F32), U32)
    return (bits[:, :w] >> 16) | (bits[:, w:] & jnp.uint32(0xFFFF0000))


def _unpack_pairs_f32(p):
    return lax.bitcast_convert_type(p << 16, F32), lax.bitcast_convert_type(p & jnp.uint32(0xFFFF0000), F32)


def _unpack_bf16_pairs(p):
    lo, hi = _unpack_pairs_f32(p)
    return lo.astype(BF16), hi.astype(BF16)


def _merge_kernel(x_ref, sc1_ref, sh1_ref, g1_ref, sc2_ref, sh2_ref,
                  o0_ref, o1_ref, o2_ref, l0_ref, l1_ref, l2_ref, mo_ref,
                  wg_ref, wpa_ref, wpm_ref, wout_ref, lng_ref, lnb_ref, x1_ref, h2p_ref, tok_sc):
    x = x_ref[...]
    d = x.shape[1]
    h = (x * (1.0 + sc1_ref[0]) + sh1_ref[0]).astype(BF16)
    g = jnp.dot(h, wg_ref[...], preferred_element_type=F32)

    def tokens(ref):
        dil, rows, _ = ref.shape
        if dil == 1:
            return ref[0]
        c = tok_sc.shape[0]
        for r in range(dil):
            for j in range(c):
                tok_sc[j, pl.ds(r, rows, stride=dil), :] = ref[r, :, j * LANES:(j + 1) * LANES]
        return jnp.concatenate([tok_sc[j] for j in range(c)], axis=1)

    l0, l1, l2 = tokens(l0_ref), tokens(l1_ref), tokens(l2_ref)
    lm = jnp.maximum(jnp.maximum(l0, l1), l2)
    e0, e1, e2 = jnp.exp(l0 - lm), jnp.exp(l1 - lm), jnp.exp(l2 - lm)
    ao = (e0 * tokens(o0_ref) + e1 * tokens(o1_ref) + e2 * tokens(o2_ref)) / (e0 + e1 + e2)
    pa = jnp.dot(ao.astype(BF16), wpa_ref[...], preferred_element_type=F32)
    pm = jnp.dot(mo_ref[...].astype(BF16), wpm_ref[...], preferred_element_type=F32)
    merged = jax.nn.sigmoid(g[:, :d]) * pa + jax.nn.sigmoid(g[:, d:]) * pm
    y = jnp.dot(merged.astype(BF16), wout_ref[...], preferred_element_type=F32)
    x1 = _layer_norm(ALPHA * x + g1_ref[0] * y, lng_ref[...], lnb_ref[...])
    x1_ref[...] = x1
    h2p_ref[...] = _pack_bf16_pairs(x1 * (1.0 + sc2_ref[0]) + sh2_ref[0])


def _merge(x, mods, attn, mo, wg, wpa, wpm, wout, lng, lnb, tm, tiles_per_batch, name):
    n, d = x.shape
    tok = lambda width: pl.BlockSpec((tm, width), lambda i: (i, 0))
    const = lambda a: pl.BlockSpec(a.shape, lambda i: (0,) * a.ndim)

    def lattice(a):
        dil, lat = a.shape[1], a.shape[2]
        per_batch = dil * lat // tm
        return pl.BlockSpec((None, dil, tm // dil, a.shape[3]), lambda i: (i // per_batch, 0, i % per_batch, 0))

    return pl.pallas_call(
        _merge_kernel,
        grid=(n // tm,),
        in_specs=[tok(d)] + [_mod_spec(m, tm, tiles_per_batch) for m in mods]
        + [lattice(a) for a in attn] + [tok(mo.shape[1])]
        + [const(a) for a in (wg, wpa, wpm, wout, lng, lnb)],
        out_specs=[tok(d), tok(d // 2)],
        out_shape=[SDS((n, d), F32), SDS((n, d // 2), U32)],
        scratch_shapes=[pltpu.VMEM((GROUP_WIDTH // LANES, tm, LANES), F32)],
        compiler_params=_cp("parallel"),
        name=name,
    )(x, *mods, *attn, mo, wg, wpa, wpm, wout, lng, lnb)


def _first_index_of_max(x, idx, big):
    mx = jnp.max(x, axis=0, keepdims=True)
    first = jnp.min(jnp.where(x == mx, idx, big), axis=0, keepdims=True)
    return mx, first


def _route_kernel(hp_ref, wlo_ref, whi_ref, bias_ref, idx_ref, w_ref, rank_ref, cnt_ref):
    tm = hp_ref.shape[0]
    ne = N_EXPERTS
    per = ne // N_EXPERT_GROUPS
    lo, hi = _unpack_bf16_pairs(hp_ref[...])
    logits = _nt_dot(wlo_ref[...], lo) + _nt_dot(whi_ref[...], hi)
    scores = jax.nn.sigmoid(logits)
    sel = scores + bias_ref[...]
    erow = lax.broadcasted_iota(I32, (ne, tm), 0).astype(F32)
    prow = lax.broadcasted_iota(I32, (per, tm), 0).astype(F32)
    gs = []
    for g in range(N_EXPERT_GROUPS):
        xg = sel[g * per:(g + 1) * per, :]
        m1, i1 = _first_index_of_max(xg, prow, per)
        m2 = jnp.max(jnp.where(prow == i1, NEG_INF, xg), axis=0, keepdims=True)
        gs.append(m1 + m2)
    gs = jnp.concatenate(gs, axis=0)
    grow = lax.broadcasted_iota(I32, gs.shape, 0).astype(F32)
    gkeep = jnp.zeros(gs.shape, F32)
    for _ in range(TOPK_GROUPS):
        _, gi = _first_index_of_max(gs, grow, N_EXPERT_GROUPS)
        pick = grow == gi
        gkeep = jnp.where(pick, 1.0, gkeep)
        gs = jnp.where(pick, NEG_INF, gs)
    keep = jnp.concatenate([jnp.broadcast_to(gkeep[g:g + 1, :], (per, tm)) for g in range(N_EXPERT_GROUPS)], axis=0)
    cand = jnp.where(keep > 0, sel, NEG_INF)
    member = jnp.zeros((ne, tm), F32)
    picks, idxs, ws = [], [], []
    for _ in range(TOP_K):
        _, ei = _first_index_of_max(cand, erow, ne)
        pick = erow == ei
        picks.append(pick)
        idxs.append(ei)
        ws.append(jnp.sum(jnp.where(pick, scores, 0.0), axis=0, keepdims=True))
        cand = jnp.where(pick, NEG_INF, cand)
        member = jnp.where(pick, 1.0, member)
    wsum = ws[0]
    for wk in ws[1:]:
        wsum = wsum + wk
    idx_ref[...] = jnp.concatenate(idxs, axis=0).astype(I32)
    w_ref[...] = jnp.concatenate(ws, axis=0) / wsum * ROUTED_SCALE
    ti = lax.broadcasted_iota(I32, (tm, tm), 0)
    tj = lax.broadcasted_iota(I32, (tm, tm), 1)
    before = (ti < tj).astype(BF16)
    mb = member.astype(BF16)
    prefix = jnp.dot(mb, before, preferred_element_type=F32)
    rank_ref[...] = jnp.concatenate(
        [jnp.sum(jnp.where(p, prefix, 0.0), axis=0, keepdims=True) for p in picks], axis=0).astype(I32)
    cnt_ref[...] = _nt_dot(jnp.ones((8, tm), BF16), mb).astype(I32)


def _route(h2p, w_router_t, bias_col):
    n, half = h2p.shape
    tm = ROUTE_TILE
    nt = n // tm
    wlo = w_router_t[:, :half]
    whi = w_router_t[:, half:]
    pair = lambda dt: SDS((TOP_K, n), dt)
    return pl.pallas_call(
        _route_kernel,
        grid=(nt,),
        in_specs=[pl.BlockSpec((tm, half), lambda i: (i, 0)),
                  pl.BlockSpec(wlo.shape, lambda i: (0, 0)),
                  pl.BlockSpec(whi.shape, lambda i: (0, 0)),
                  pl.BlockSpec(bias_col.shape, lambda i: (0, 0))],
        out_specs=[pl.BlockSpec((TOP_K, tm), lambda i: (0, i))] * 3
        + [pl.BlockSpec((None, 8, N_EXPERTS), lambda i: (i, 0, 0))],
        out_shape=[pair(I32), pair(F32), pair(I32), SDS((nt, 8, N_EXPERTS), I32)],
        compiler_params=_cp("parallel"),
        name="route",
    )(h2p, wlo, whi, bias_col)


def _dest_kernel(idx_ref, rank_ref, base_ref, dest_ref):
    tm = idx_ref.shape[1]
    erow = lax.broadcasted_iota(I32, (N_EXPERTS, tm), 0)
    base = base_ref[...]
    idx = idx_ref[...]
    rows = [jnp.sum(jnp.where(erow == idx[k:k + 1, :], base, 0.0), axis=0, keepdims=True) for k in range(TOP_K)]
    dest = jnp.concatenate(rows, axis=0).astype(I32) + rank_ref[...]
    tile = dest_ref.shape[2]
    for j in range(dest_ref.shape[0]):
        dest_ref[j] = dest[:, j * tile:(j + 1) * tile]


def _dest(idx, rank, base_cols, tile):
    n = idx.shape[1]
    per_route = ROUTE_TILE // tile
    return pl.pallas_call(
        _dest_kernel,
        grid=(n // ROUTE_TILE,),
        in_specs=[pl.BlockSpec((TOP_K, ROUTE_TILE), lambda i: (0, i)),
                  pl.BlockSpec((TOP_K, ROUTE_TILE), lambda i: (0, i)),
                  pl.BlockSpec((None, N_EXPERTS, 1), lambda i: (i, 0, 0))],
        out_specs=pl.BlockSpec((per_route, TOP_K, tile), lambda i: (i, 0, 0)),
        out_shape=SDS((n // tile, TOP_K, tile), I32),
        compiler_params=_cp("parallel"),
        name="dest",
    )(idx, rank, base_cols)


def _row_copy_loop(tile, make_copy, start):
    def body(t, carry):
        for k in range(TOP_K):
            cp = make_copy(k, t)
            if start:
                cp.start(priority=k % 2)
            else:
                cp.wait()
        return carry
    lax.fori_loop(0, tile, body, 0, unroll=8)


def _dest_fetch(dest_hbm, tile_idx, dest_smem, sem_idx):
    n_idx = dest_smem.shape[0]
    start = tile_idx * n_idx
    if not isinstance(start, int):
        start = pl.multiple_of(start, n_idx)
    return pltpu.make_async_copy(dest_hbm.at[pl.ds(start, n_idx)], dest_smem, sem_idx)


def _rows_to_pieces(x, ref):
    c = x.shape[1] // LANES
    for j in range(c):
        ref[pl.ds(j, x.shape[0], stride=c), :] = x[:, j * LANES:(j + 1) * LANES]


def _pieces_to_rows(ref, rows, c):
    return jnp.concatenate([ref[pl.ds(j, rows, stride=c), :] for j in range(c)], axis=1)


def _piece(ref, row, c):
    return ref.at[pl.ds(pl.multiple_of(row * c, c), c)]


def _scatter_kernel(dest_hbm, hp_ref, xs_in, xs_out, stage, dest_smem, sem_idx, sem_rows):
    del xs_in
    i = pl.program_id(0)
    n = pl.num_programs(0)
    tile = hp_ref.shape[0]
    c = hp_ref.shape[1] // LANES
    slot = lax.rem(i, 2)
    idx_cp = _dest_fetch(dest_hbm, i, dest_smem, sem_idx)
    idx_cp.start()

    def row_copy(k, t):
        return pltpu.make_async_copy(_piece(stage.at[slot], t, c), _piece(xs_out, dest_smem[k * tile + t], c),
                                     sem_rows.at[slot])

    def wait_tile(s):
        _row_copy_loop(tile, lambda k, t: pltpu.make_async_copy(
            stage.at[s, pl.ds(0, c)], xs_out.at[pl.ds(0, c)], sem_rows.at[s]), False)

    @pl.when(i >= 2)
    def _():
        wait_tile(slot)

    _rows_to_pieces(hp_ref[...], stage.at[slot])
    idx_cp.wait()
    _row_copy_loop(tile, row_copy, True)

    @pl.when(i == n - 1)
    def _():
        @pl.when(n >= 2)
        def _():
            wait_tile(1 - slot)
        wait_tile(slot)


def _scatter_rows(dest_flat, h2p, n_rows):
    n, half = h2p.shape
    tile = MOE_TILE
    c = half // LANES
    xs0 = jnp.zeros((n_rows * c, LANES), U32)
    return pl.pallas_call(
        _scatter_kernel,
        grid=(n // tile,),
        in_specs=[pl.BlockSpec(memory_space=pl.ANY),
                  pl.BlockSpec((tile, half), lambda i: (i, 0)),
                  pl.BlockSpec(memory_space=pl.ANY)],
        out_specs=pl.BlockSpec(memory_space=pl.ANY),
        out_shape=SDS((n_rows * c, LANES), U32),
        scratch_shapes=[pltpu.VMEM((2, tile * c, LANES), U32), pltpu.SMEM((TOP_K * tile,), I32),
                        pltpu.SemaphoreType.DMA, pltpu.SemaphoreType.DMA((2,))],
        input_output_aliases={2: 0},
        compiler_params=_cp("arbitrary"),
        name="scatter_rows",
    )(dest_flat, h2p, xs0)


def _expert_kernel(blk_exp_ref, n_used_ref, xs_ref, win_ref, wout_ref, ys_ref, win_sc, wout_sc):
    i = pl.program_id(0)
    prev = blk_exp_ref[jnp.maximum(i - 1, 0)]

    @pl.when((i == 0) | (blk_exp_ref[i] != prev))
    def _():
        win_sc[...] = win_ref[...].astype(BF16)
        wout_sc[...] = wout_ref[...].astype(BF16)

    @pl.when(i < n_used_ref[0])
    def _():
        half = win_sc.shape[0] // 2
        lo, hi = _unpack_bf16_pairs(_pieces_to_rows(xs_ref, MOE_BLOCK, half // LANES))
        ag = (jnp.dot(lo, win_sc[0:half, :], preferred_element_type=F32)
              + jnp.dot(hi, win_sc[half:2 * half, :], preferred_element_type=F32))
        a = ag[:, :D_EXPERT]
        g = ag[:, D_EXPERT:]
        mid = (a * jax.nn.sigmoid(a) * g).astype(BF16)
        _rows_to_pieces(_pack_bf16_pairs(jnp.dot(mid, wout_sc[...], preferred_element_type=F32)), ys_ref)

    @pl.when(i >= n_used_ref[0])
    def _():
        ys_ref[...] = jnp.zeros(ys_ref.shape, ys_ref.dtype)


def _experts(xs, blk_exp, n_used, w_exp_in, w_exp_out):
    d = w_exp_in.shape[1]
    cx = cy = d // 2 // LANES
    n_rows = xs.shape[0] // cx
    nb = n_rows // MOE_BLOCK
    grid_spec = pltpu.PrefetchScalarGridSpec(
        num_scalar_prefetch=2,
        grid=(nb,),
        in_specs=[pl.BlockSpec((MOE_BLOCK * cx, LANES), lambda i, be, nu: (i, 0)),
                  pl.BlockSpec((None, d, 2 * D_EXPERT), lambda i, be, nu: (be[i], 0, 0)),
                  pl.BlockSpec((None, D_EXPERT, d), lambda i, be, nu: (be[i], 0, 0))],
        out_specs=pl.BlockSpec((MOE_BLOCK * cy, LANES), lambda i, be, nu: (i, 0)),
        scratch_shapes=[pltpu.VMEM((d, 2 * D_EXPERT), BF16), pltpu.VMEM((D_EXPERT, d), BF16)],
    )
    return pl.pallas_call(
        _expert_kernel,
        grid_spec=grid_spec,
        out_shape=SDS((n_rows * cy, LANES), U32),
        compiler_params=_cp("arbitrary"),
        name="experts",
    )(blk_exp, n_used, xs, w_exp_in, w_exp_out)


def _final_kernel(dest_hbm, x1_ref, hp_ref, w_ref, g2_ref, ys_hbm, wsin_ref, wsout_ref, lng_ref, lnb_ref,
                  y_ref, rows_sc, dest_smem, sem_idx, sem_rows, *, tile_offset):
    i = pl.program_id(0)
    n = pl.num_programs(0)
    tile = x1_ref.shape[0]
    c = hp_ref.shape[1] // LANES
    slot = lax.rem(i, 2)

    def start_rows(s):
        _row_copy_loop(tile, lambda k, t: pltpu.make_async_copy(
            _piece(ys_hbm, dest_smem[k * tile + t], c), _piece(rows_sc.at[s, k], t, c), sem_rows.at[s]), True)

    @pl.when(i == 0)
    def _():
        first = _dest_fetch(dest_hbm, tile_offset, dest_smem, sem_idx)
        first.start()
        first.wait()
        start_rows(0)

    nxt = _dest_fetch(dest_hbm, jnp.minimum(i + 1, n - 1) + tile_offset, dest_smem, sem_idx)

    @pl.when(i + 1 < n)
    def _():
        nxt.start()

    half = hp_ref.shape[1]
    lo, hi = _unpack_bf16_pairs(hp_ref[...])
    ag = (jnp.dot(lo, wsin_ref[0:half, :], preferred_element_type=F32)
          + jnp.dot(hi, wsin_ref[half:2 * half, :], preferred_element_type=F32))
    a = ag[:, :D_SHARED]
    g = ag[:, D_SHARED:]
    y2 = jnp.dot((a * jax.nn.sigmoid(a) * g).astype(BF16), wsout_ref[...], preferred_element_type=F32)
    w_cols = jnp.concatenate([w_ref[...], jnp.zeros((tile - TOP_K, tile), F32)], axis=0).T

    @pl.when(i + 1 < n)
    def _():
        nxt.wait()
        start_rows(1 - slot)

    _row_copy_loop(tile, lambda k, t: pltpu.make_async_copy(
        ys_hbm.at[pl.ds(0, c)], rows_sc.at[slot, 0, pl.ds(0, c)], sem_rows.at[slot]), False)
    y_lo, y_hi = y2[:, :half], y2[:, half:]
    for k in range(TOP_K):
        r_lo, r_hi = _unpack_pairs_f32(_pieces_to_rows(rows_sc.at[slot, k], tile, c))
        y_lo = y_lo + w_cols[:, k:k + 1] * r_lo
        y_hi = y_hi + w_cols[:, k:k + 1] * r_hi
    y2 = jnp.concatenate([y_lo, y_hi], axis=1)
    y_ref[...] = _layer_norm(ALPHA * x1_ref[...] + g2_ref[0] * y2, lng_ref[...], lnb_ref[...])


def _final(dest_flat, x1, h2p, w_top, g2_3, ys, wsin, wsout, lng, lnb, tile_offset, tiles_per_batch, name):
    n, d = x1.shape
    tile = MOE_TILE
    assert tile == LANES
    const = lambda a: pl.BlockSpec(a.shape, lambda i: (0,) * a.ndim)
    return pl.pallas_call(
        functools.partial(_final_kernel, tile_offset=tile_offset),
        grid=(n // tile,),
        in_specs=[pl.BlockSpec(memory_space=pl.ANY),
                  pl.BlockSpec((tile, d), lambda i: (i, 0)),
                  pl.BlockSpec((tile, d // 2), lambda i: (i, 0)),
                  pl.BlockSpec((TOP_K, tile), lambda i: (0, i)),
                  _mod_spec(g2_3, tile, tiles_per_batch),
                  pl.BlockSpec(memory_space=pl.ANY),
                  const(wsin), const(wsout), const(lng), const(lnb)],
        out_specs=pl.BlockSpec((tile, d), lambda i: (i, 0)),
        out_shape=SDS((n, d), F32),
        scratch_shapes=[pltpu.VMEM((2, TOP_K, tile * (d // 2 // LANES), LANES), U32), pltpu.SMEM((TOP_K * tile,), I32),
                        pltpu.SemaphoreType.DMA, pltpu.SemaphoreType.DMA((2,))],
        compiler_params=_cp("arbitrary"),
        name=name,
    )(dest_flat, x1, h2p, w_top, g2_3, ys, wsin, wsout, lng, lnb)


def _sorted_layout(counts):
    totals = jnp.sum(counts, axis=0)
    padded = (totals + MOE_BLOCK - 1) // MOE_BLOCK * MOE_BLOCK
    pends = jnp.cumsum(padded)
    base = (pends - padded)[None, :] + jnp.cumsum(counts, axis=0) - counts
    return base.astype(I32), pends


def _moe_routed(h2p, w_router, router_bias, w_exp_in, w_exp_out):
    n = h2p.shape[0]
    idx, w_top, rank, cnt = _route(h2p, w_router.T.astype(BF16), router_bias.reshape(N_EXPERTS, 1))
    base, pends = _sorted_layout(cnt[:, 0, :])
    n_blocks = -(-(n * TOP_K + N_EXPERTS * (MOE_BLOCK - 1)) // MOE_BLOCK)
    blk_start = jnp.arange(n_blocks, dtype=I32) * MOE_BLOCK
    blk_exp = jnp.minimum(jnp.sum((pends[None, :] <= blk_start[:, None]).astype(I32), axis=1), N_EXPERTS - 1)
    n_used = (pends[-1:] // MOE_BLOCK).astype(I32)
    dest = _dest(idx, rank, base.astype(F32)[:, :, None], MOE_TILE).reshape(-1)
    xs = _scatter_rows(dest, h2p, n_blocks * MOE_BLOCK)
    ys = _experts(xs, blk_exp, n_used, w_exp_in, w_exp_out)
    return dest, w_top, ys


def _pad_heads(w, nh, dk):
    d = w.shape[0]
    return jnp.pad(w.reshape(d, nh, dk), ((0, 0), (0, 0), (0, DK_PAD - dk))).reshape(d, nh * DK_PAD)


def _split_w_in(w_in):
    aw = N_GROUPS * GROUP_WIDTH
    nh, dk, dv = MLSTM_HEADS, MLSTM_DK, MLSTM_DV
    offs = np.cumsum([0, aw, aw, aw, nh * dk, nh * dk, nh * dv, nh * dv, nh, nh, w_in.shape[0], w_in.shape[0]])
    seg = [w_in[:, offs[i]:offs[i + 1]] for i in range(11)]
    q_a, k_a, v_a, q_m, k_m, v_m, o_m, i_m, f_m, g_a, g_b = seg
    bf = lambda a: a.astype(BF16)
    w_gates = jnp.pad(jnp.concatenate([i_m, f_m], axis=1), ((0, 0), (0, LANES - 2 * nh)))
    return dict(qa=bf(q_a), kva=bf(jnp.concatenate([k_a, v_a], axis=1)),
                qm=bf(_pad_heads(q_m, nh, dk)), km=bf(_pad_heads(k_m, nh, dk)), vm=bf(v_m), om=bf(o_m),
                gates=w_gates, g=bf(jnp.concatenate([g_a, g_b], axis=1)))


def _mixing(x, mods, tm, tiles_per_batch, batch, seq, caches, states, wts, prompt):
    n, d = x.shape
    sc1, sh1, g1 = mods["scale1"], mods["shift1"], mods["gate1"]
    sc2, sh2 = mods["scale2"], mods["shift2"]
    act = BF16 if prompt else F32
    if prompt:
        tail = min(max(w for w, _ in ATTN_GROUPS), seq)
        qkv_l, kv_tail = _proj_lattice(x, sc1, sh1, wts["qa"], wts["kva"], batch, seq, tm, tail)
    else:
        q_a, kv_a = _proj(x, sc1, sh1, [wts["qa"], wts["kva"]], [F32, F32], None, tm, tiles_per_batch, "proj_attn_s")
    qm, km, vm, om, gates = _proj(x, sc1, sh1, [wts["qm"], wts["km"], wts["vm"], wts["om"]],
                                  [act, act, act, F32], wts["gates"], tm, tiles_per_batch,
                                  "proj_mlstm_p" if prompt else "proj_mlstm_s")
    attn_o, attn_l, bufs = [], [], []
    for g, (window, dilation) in enumerate(ATTN_GROUPS):
        if prompt:
            o, l = _band_attn(*qkv_l[g], window)
            keep = min(window, seq)
            kv3 = kv_tail.reshape(batch, tail, 2, N_GROUPS, GROUP_WIDTH)[:, tail - keep:, :, g, :]
            bufs.append(kv3.reshape(1, batch, keep, 2, ATTN_HEADS, ATTN_HEAD_DIM))
        else:
            o, l, nc = _cache_attn(q_a.reshape(batch, seq, -1), kv_a.reshape(batch, seq, -1), caches[g], g,
                                   window, dilation)
            o, l = o[None, None], l[None, None]
            bufs.append(nc[None])
        attn_o.append(o)
        attn_l.append(l)
    r3 = lambda a: a.reshape(batch, seq, a.shape[-1])
    n_valid = MLSTM_CHUNK if prompt else seq
    mo, c1, n1, m1 = _mlstm(r3(qm), r3(km), r3(vm), r3(om), r3(gates), wts["bif"], wts["norm_g"],
                            states[0], states[1], states[2], n_valid, act,
                            "mlstm_p" if prompt else "mlstm_s")
    x1, h2p = _merge(x, [sc1, sh1, g1, sc2, sh2], attn_o + attn_l, mo.reshape(n, -1),
                     wts["g"], wts["pa"], wts["pm"], wts["out"], wts["ln1_g"], wts["ln1_b"],
                     tm, tiles_per_batch, "merge_p" if prompt else "merge_s")
    return x1, h2p, bufs, (c1[None], n1[None], m1[None])


def _mod_pieces(mod, d, rows_per_batch, tm):
    names = ("shift1", "scale1", "gate1", "shift2", "scale2", "gate2")
    out = {}
    for p, name in enumerate(names):
        piece = mod[:, p * d:(p + 1) * d]
        if rows_per_batch % tm == 0:
            out[name] = piece[:, None, :]
        else:
            out[name] = jnp.repeat(piece, rows_per_batch, axis=0).reshape(-1, tm, d)
    return out


def kernel(x_prompt, x_sample, cache_kv_w128, cache_kv_w512, cache_kv_w2048, state_mlstm_C, state_mlstm_n, state_mlstm_m, c_prompt, c_sample, w_ada, b_ada, w_in, b_if, mlstm_norm_g, w_proj_attn, w_proj_mlstm, w_out, ln1_g, ln1_b, w_router, router_bias, w_exp_in, w_exp_out, w_sh_in, w_sh_out, ln2_g, ln2_b):
    assert w_ada.shape[0] == DEPTH
    bp, sp, d = x_prompt.shape
    bs, ss, _ = x_sample.shape
    nh = MLSTM_HEADS
    np_, ns = bp * sp, bs * ss

    mod = _ada(jnp.concatenate([c_prompt, c_sample], axis=0), w_ada[0], b_ada[0])
    wts = _split_w_in(w_in[0])
    wts.update(
        bif=jnp.pad(b_if[0], (0, LANES - 2 * nh)).reshape(1, LANES),
        norm_g=mlstm_norm_g[0].reshape(1, -1),
        pa=w_proj_attn[0].astype(BF16), pm=w_proj_mlstm[0].astype(BF16), out=w_out[0].astype(BF16),
        ln1_g=ln1_g[0].reshape(1, d), ln1_b=ln1_b[0].reshape(1, d))

    tm_p, tm_s = 512, 256
    mods_p = _mod_pieces(mod[:bp], d, sp, tm_p)
    mods_s = _mod_pieces(mod[bp:], d, ss, tm_s)
    zeros_p = (jnp.zeros((bp, nh, MLSTM_DK, MLSTM_DV), F32), jnp.zeros((bp, nh, MLSTM_DK), F32),
               jnp.zeros((bp, nh), F32))
    x1p, h2p_p, bufs_p, st_p = _mixing(x_prompt.reshape(np_, d), mods_p, tm_p, sp // tm_p, bp, sp, None,
                                       zeros_p, wts, True)
    caches = (cache_kv_w128[0], cache_kv_w512[0], cache_kv_w2048[0])
    states = (state_mlstm_C[0], state_mlstm_n[0], state_mlstm_m[0])
    x1s, h2p_s, bufs_s, st_s = _mixing(x_sample.reshape(ns, d), mods_s, tm_s, 1, bs, ss, caches, states, wts, False)

    h2p = jnp.concatenate([h2p_p, h2p_s], axis=0)
    dest, w_top, ys = _moe_routed(h2p, w_router[0], router_bias[0], w_exp_in[0], w_exp_out[0])
    wsin, wsout = w_sh_in[0].astype(BF16), w_sh_out[0].astype(BF16)
    lng, lnb = ln2_g[0].reshape(1, d), ln2_b[0].reshape(1, d)
    g2_p = _mod_pieces(mod[:bp], d, sp, MOE_TILE)["gate2"]
    g2_s = _mod_pieces(mod[bp:], d, ss, MOE_TILE)["gate2"]
    y_p = _final(dest, x1p, h2p_p, w_top[:, :np_], g2_p, ys, wsin, wsout, lng, lnb, 0, sp // MOE_TILE, "final_p")
    y_s = _final(dest, x1s, h2p_s, w_top[:, np_:], g2_s, ys, wsin, wsout, lng, lnb, np_ // MOE_TILE, 1, "final_s")

    return (y_p.reshape(bp, sp, d), y_s.reshape(bs, ss, d),
            bufs_p[0], bufs_p[1], bufs_p[2], st_p[0], st_p[1], st_p[2],
            bufs_s[0], bufs_s[1], bufs_s[2], st_s[0], st_s[1], st_s[2])
```

```python
import functools

import numpy as np
import jax
import jax.numpy as jnp
from jax import lax
from jax.experimental import pallas as pl
from jax.experimental.pallas import tpu as pltpu

F32 = jnp.float32
BF16 = jnp.bfloat16
I32 = jnp.int32
U32 = jnp.uint32
HIGHEST = lax.Precision.HIGHEST
SDS = jax.ShapeDtypeStruct
NEG_INF = float("-inf")

ATTN_GROUPS = ((128, 1), (512, 4), (2048, 16))
N_GROUPS = 3
ATTN_HEADS = 4
ATTN_HEAD_DIM = 64
GROUP_WIDTH = ATTN_HEADS * ATTN_HEAD_DIM
ATTN_BLOCK = 128
MLSTM_HEADS = 8
MLSTM_DK = 64
MLSTM_DV = 128
N_EXPERTS = 256
TOP_K = 8
N_EXPERT_GROUPS = 8
TOPK_GROUPS = 4
D_EXPERT = 256
D_SHARED = 256
ROUTED_SCALE = 2.5
DEPTH = 1
ALPHA = (2 * DEPTH) ** 0.25
EPS = 1e-5

LANES = 128
MLSTM_CHUNK = 128
DK_PAD = LANES
MOE_TILE = 128
MOE_BLOCK = 512
ROUTE_TILE = 512
VMEM_LIMIT = 56 << 20


def _cp(*sem, vmem=VMEM_LIMIT):
    return pltpu.CompilerParams(dimension_semantics=sem, vmem_limit_bytes=vmem)


def _nt_dot(a, b):
    return lax.dot_general(a, b, (((1,), (1,)), ((), ())), preferred_element_type=F32)


def _layer_norm(r, g, b):
    mu = jnp.mean(r, axis=-1, keepdims=True)
    var = jnp.mean(jnp.square(r - mu), axis=-1, keepdims=True)
    return (r - mu) * lax.rsqrt(var + EPS) * g + b


def _mod_spec(mod3, tm, tiles_per_batch):
    d = mod3.shape[-1]
    if mod3.shape[1] == 1:
        return pl.BlockSpec((1, 1, d), lambda i: (i // tiles_per_batch, 0, 0))
    return pl.BlockSpec((1, tm, d), lambda i: (i, 0, 0))


def _ada_kernel(c_ref, w_ref, b_ref, o_ref):
    c = c_ref[...]
    s = c * jax.nn.sigmoid(c)
    o_ref[...] = jnp.dot(s, w_ref[...], precision=HIGHEST, preferred_element_type=F32) + b_ref[...]


def _ada(c_all, w_ada, b_ada):
    r, d = c_all.shape
    n = w_ada.shape[1]
    tn = 1024
    return pl.pallas_call(
        _ada_kernel,
        grid=(n // tn,),
        in_specs=[pl.BlockSpec((r, d), lambda j: (0, 0)),
                  pl.BlockSpec((d, tn), lambda j: (0, j)),
                  pl.BlockSpec((1, tn), lambda j: (0, j))],
        out_specs=pl.BlockSpec((r, tn), lambda j: (0, j)),
        out_shape=SDS((r, n), F32),
        compiler_params=_cp("parallel"),
        name="ada",
    )(c_all, w_ada, b_ada.reshape(1, n))


def _proj_kernel(n_w, has_hp, x_ref, sc_ref, sh_ref, *refs):
    w_refs = refs[:n_w]
    pos = n_w
    if has_hp:
        whp_ref = refs[pos]
        pos += 1
    o_refs = refs[pos:pos + n_w]
    h = x_ref[...] * (1.0 + sc_ref[0]) + sh_ref[0]
    hb = h.astype(BF16)
    for w_ref, o_ref in zip(w_refs, o_refs):
        o_ref[...] = jnp.dot(hb, w_ref[...], preferred_element_type=F32).astype(o_ref.dtype)
    if has_hp:
        ohp_ref = refs[pos + n_w]
        wn = ohp_ref.shape[1]
        h_lo = (h - hb.astype(F32)).astype(BF16)
        r = jnp.dot(hb, whp_ref[...], preferred_element_type=F32)
        ohp_ref[...] = (r[:, :wn] + r[:, wn:]) + jnp.dot(h_lo, whp_ref[:, 0:wn], preferred_element_type=F32)


def _proj(x, sc3, sh3, weights, out_dtypes, w_hp, tm, tiles_per_batch, name):
    n, d = x.shape
    n_w = len(weights)
    has_hp = w_hp is not None
    in_specs = [pl.BlockSpec((tm, d), lambda i: (i, 0)),
                _mod_spec(sc3, tm, tiles_per_batch), _mod_spec(sh3, tm, tiles_per_batch)]
    in_specs += [pl.BlockSpec(w.shape, lambda i: (0, 0)) for w in weights]
    out_specs = [pl.BlockSpec((tm, w.shape[1]), lambda i: (i, 0)) for w in weights]
    out_shape = [SDS((n, w.shape[1]), dt) for w, dt in zip(weights, out_dtypes)]
    args = [x, sc3, sh3, *weights]
    if has_hp:
        in_specs.append(pl.BlockSpec(w_hp.shape, lambda i: (0, 0)))
        out_specs.append(pl.BlockSpec((tm, w_hp.shape[1] // 2), lambda i: (i, 0)))
        out_shape.append(SDS((n, w_hp.shape[1] // 2), F32))
        args.append(w_hp)
    return pl.pallas_call(
        functools.partial(_proj_kernel, n_w, has_hp),
        grid=(n // tm,),
        in_specs=in_specs, out_specs=out_specs, out_shape=out_shape,
        compiler_params=_cp("parallel"),
        name=name,
    )(*args)


def _head_masks(width):
    lane_head = lax.broadcasted_iota(I32, (1, width), 1) // ATTN_HEAD_DIM
    return [(lane_head == h).astype(F32) for h in range(ATTN_HEADS)]


def _band_attn_kernel(wband, q_ref, kc_ref, kp_ref, vc_ref, vp_ref, o_ref, l_ref):
    j = pl.program_id(2)
    qb = q_ref.shape[0]
    q = q_ref[...].astype(F32)
    k = jnp.concatenate([kp_ref[...], kc_ref[...]], axis=0).astype(BF16)
    v = jnp.concatenate([vp_ref[...], vc_ref[...]], axis=0).astype(BF16)
    qi = lax.broadcasted_iota(I32, (qb, 2 * qb), 0)
    kj = lax.broadcasted_iota(I32, (qb, 2 * qb), 1)
    dist = qi + qb - kj
    valid = jnp.where(dist >= 0, 1, 0) * jnp.where(dist <= wband, 1, 0) * jnp.where(j * qb + kj - qb >= 0, 1, 0)
    bias = jnp.where(valid > 0, 0.0, NEG_INF)
    hms = _head_masks(q.shape[1])
    qs = jnp.concatenate([q * (hm * ATTN_HEAD_DIM ** -0.5) for hm in hms], axis=0).astype(BF16)
    s = _nt_dot(qs, k) + jnp.concatenate([bias] * ATTN_HEADS, axis=0)
    mx = jnp.max(s, axis=-1, keepdims=True)
    p = jnp.exp(s - mx)
    den = jnp.sum(p, axis=-1, keepdims=True)
    o4 = jnp.dot((p / den).astype(BF16), v, preferred_element_type=F32)
    l4 = mx + jnp.log(den)
    o_acc = jnp.zeros(o_ref.shape, F32)
    l_acc = jnp.zeros(l_ref.shape, F32)
    for h, hm in enumerate(hms):
        o_acc = o_acc + o4[h * qb:(h + 1) * qb, :] * hm
        l_acc = l_acc + l4[h * qb:(h + 1) * qb, :] * hm
    o_ref[...] = o_acc
    l_ref[...] = l_acc


def _band_attn(q_l, k_l, v_l, window):
    batch, dilation, lat, gw = q_l.shape
    nb = lat // ATTN_BLOCK
    blk = (None, None, ATTN_BLOCK, gw)
    cur = lambda b, r, j: (b, r, j, 0)
    prev = lambda b, r, j: (b, r, jnp.maximum(j - 1, 0), 0)
    return pl.pallas_call(
        functools.partial(_band_attn_kernel, window // dilation),
        grid=(batch, dilation, nb),
        in_specs=[pl.BlockSpec(blk, cur), pl.BlockSpec(blk, cur), pl.BlockSpec(blk, prev),
                  pl.BlockSpec(blk, cur), pl.BlockSpec(blk, prev)],
        out_specs=[pl.BlockSpec(blk, cur), pl.BlockSpec(blk, cur)],
        out_shape=[SDS(q_l.shape, F32)] * 2,
        compiler_params=_cp("parallel", "parallel", "arbitrary"),
        name=f"band_attn_d{dilation}",
    )(q_l, k_l, k_l, v_l, v_l)


def _rows_to_lane_tiles(x, ref):
    for j in range(x.shape[1] // LANES):
        ref[j] = x[:, j * LANES:(j + 1) * LANES]


def _proj_lattice_kernel(dils, t0, x_ref, sc_ref, sh_ref, wq_ref, wkv_ref, *refs):
    ng = len(dils)
    lat_refs = refs[:3 * ng]
    kv_tail_ref = refs[3 * ng]
    zs = refs[3 * ng + 1]
    gw = GROUP_WIDTH
    tm = x_ref.shape[0]
    hb = (x_ref[...] * (1.0 + sc_ref[0]) + sh_ref[0]).astype(BF16)
    zq = jnp.dot(hb, wq_ref[...], preferred_element_type=F32)
    zkv = jnp.dot(hb, wkv_ref[...], preferred_element_type=F32)

    @pl.when(pl.program_id(1) >= t0)
    def _():
        kv_tail_ref[...] = zkv

    for g, d in enumerate(dils):
        for part, z in enumerate((zq[:, g * gw:(g + 1) * gw], zkv[:, g * gw:(g + 1) * gw],
                                  zkv[:, (ng + g) * gw:(ng + g + 1) * gw])):
            out = lat_refs[3 * g + part]
            if d == 1:
                out[0] = z.astype(out.dtype)
            else:
                _rows_to_lane_tiles(z, zs)
                for r in range(d):
                    out[r] = jnp.concatenate([zs[j, pl.ds(r, tm // d, stride=d), :] for j in range(gw // LANES)],
                                             axis=1).astype(out.dtype)


def _proj_lattice(x, sc3, sh3, wq, wkv, batch, seq, tm, tail):
    d_model = x.shape[1]
    tiles = seq // tm
    t0 = (seq - tail) // tm
    assert seq % tm == 0 and (seq - tail) % tm == 0
    dils = tuple(d for _, d in ATTN_GROUPS)
    gw = GROUP_WIDTH
    lat_specs, lat_shapes = [], []
    for d in dils:
        for _ in range(3):
            lat_specs.append(pl.BlockSpec((None, d, tm // d, gw), lambda b, t: (b, 0, t, 0)))
            lat_shapes.append(SDS((batch, d, seq // d, gw), BF16))
    outs = pl.pallas_call(
        functools.partial(_proj_lattice_kernel, dils, t0),
        grid=(batch, tiles),
        in_specs=[pl.BlockSpec((tm, d_model), lambda b, t: (b * tiles + t, 0)),
                  pl.BlockSpec((1, 1, d_model), lambda b, t: (b, 0, 0)),
                  pl.BlockSpec((1, 1, d_model), lambda b, t: (b, 0, 0)),
                  pl.BlockSpec(wq.shape, lambda b, t: (0, 0)),
                  pl.BlockSpec(wkv.shape, lambda b, t: (0, 0))],
        out_specs=lat_specs + [pl.BlockSpec((None, tm, wkv.shape[1]), lambda b, t: (b, jnp.maximum(t - t0, 0), 0))],
        out_shape=lat_shapes + [SDS((batch, tail, wkv.shape[1]), F32)],
        scratch_shapes=[pltpu.VMEM((gw // LANES, tm, LANES), F32)],
        compiler_params=_cp("parallel", "arbitrary"),
        name="proj_attn_p",
    )(x, sc3, sh3, wq, wkv)
    return [outs[3 * g:3 * g + 3] for g in range(len(dils))], outs[-1]


def _cache_attn_kernel(wband, dilation, c_ref, q_ref, kv_ref, o_ref, l_ref, nc_ref):
    lw = c_ref.shape[-1]
    t_new = q_ref.shape[1]
    dh = q_ref.shape[2]

    def band_bias(n_keys, key_base):
        t = lax.broadcasted_iota(I32, (t_new, n_keys), 0)
        p = lax.broadcasted_iota(I32, (t_new, n_keys), 1) + key_base
        delta = lw + t - p
        ok = (jnp.where(delta >= 0, 1, 0) * jnp.where((delta & (dilation - 1)) == 0, 1, 0)
              * jnp.where(delta <= wband * dilation, 1, 0))
        return jnp.where(ok > 0, 0.0, NEG_INF)

    bias_c = band_bias(lw, 0)
    bias_n = band_bias(t_new, lw)
    place = (lax.broadcasted_iota(I32, (t_new, LANES), 1)
             == lax.broadcasted_iota(I32, (t_new, LANES), 0) + (LANES - t_new)).astype(F32)
    tail = lax.broadcasted_iota(I32, (1, LANES), 1) >= LANES - t_new
    for h in range(ATTN_HEADS):
        qh = (q_ref[h] * dh ** -0.5).astype(BF16)
        kt, vt = c_ref[0, h], c_ref[1, h]
        knt, vnt = kv_ref[0, h], kv_ref[1, h]
        sc = jnp.dot(qh, kt.astype(BF16), preferred_element_type=F32) + bias_c
        sn = jnp.dot(qh, knt.astype(BF16), preferred_element_type=F32) + bias_n
        mx = jnp.maximum(jnp.max(sc, axis=-1, keepdims=True), jnp.max(sn, axis=-1, keepdims=True))
        pc = jnp.exp(sc - mx)
        pn = jnp.exp(sn - mx)
        den = jnp.sum(pc, axis=-1, keepdims=True) + jnp.sum(pn, axis=-1, keepdims=True)
        o_ref[h] = (_nt_dot((pc / den).astype(BF16), vt.astype(BF16))
                    + _nt_dot((pn / den).astype(BF16), vnt.astype(BF16)))
        l_ref[h] = jnp.broadcast_to(mx + jnp.log(den), (t_new, dh))
        for kv, (old, new) in enumerate(((kt, knt), (vt, vnt))):
            rolled = pltpu.roll(old, lw - t_new, axis=1)
            new_tile = jnp.dot(new, place, precision=HIGHEST, preferred_element_type=F32)
            if lw > LANES:
                nc_ref[kv, h, :, 0:lw - LANES] = rolled[:, 0:lw - LANES]
            nc_ref[kv, h, :, lw - LANES:lw] = jnp.where(tail, new_tile, rolled[:, lw - LANES:lw])


def _cache_attn(q_s, kv_s, cache, g, window, dilation):
    b, lw = cache.shape[0], cache.shape[1]
    t_new = q_s.shape[1]
    nh, dh = ATTN_HEADS, ATTN_HEAD_DIM
    assert lw == window and lw % LANES == 0 and t_new % 8 == 0 and dilation & (dilation - 1) == 0
    ct = jnp.transpose(cache, (0, 2, 3, 4, 1))
    qh = jnp.transpose(q_s.reshape(b, t_new, N_GROUPS, nh, dh)[:, :, g], (0, 2, 1, 3))
    kvt = jnp.transpose(kv_s.reshape(b, t_new, 2, N_GROUPS, nh, dh)[:, :, :, g], (0, 2, 3, 4, 1))
    win = pl.BlockSpec((None, 2, nh, dh, lw), lambda i: (i, 0, 0, 0, 0))
    per_q = pl.BlockSpec((None, nh, t_new, dh), lambda i: (i, 0, 0, 0))
    o, l, nc = pl.pallas_call(
        functools.partial(_cache_attn_kernel, window // dilation, dilation),
        grid=(b,),
        in_specs=[win, per_q, pl.BlockSpec((None, 2, nh, dh, t_new), lambda i: (i, 0, 0, 0, 0))],
        out_specs=[per_q, per_q, win],
        out_shape=[SDS((b, nh, t_new, dh), F32), SDS((b, nh, t_new, dh), F32), SDS((b, 2, nh, dh, lw), F32)],
        compiler_params=_cp("parallel"),
        name=f"cache_attn_d{dilation}",
    )(ct, qh, kvt)
    tok = lambda a: jnp.transpose(a, (0, 2, 1, 3)).reshape(b * t_new, nh * dh)
    return tok(o), tok(l), jnp.transpose(nc, (0, 4, 1, 2, 3))


def _log_sigmoid(x):
    return jnp.minimum(x, 0.0) - jnp.log1p(jnp.exp(-jnp.abs(x)))


def _mlstm_kernel(n_valid, q_ref, k_ref, v_ref, o_ref, g_ref, bif_ref, ng_ref, c0_ref, n0_ref, m0_ref,
                  h_ref, c1_ref, n1_ref, m1_ref, c_sc, n_sc, m_sc):
    ci = pl.program_id(1)
    nh, dk, dv = MLSTM_HEADS, MLSTM_DK, MLSTM_DV
    L = MLSTM_CHUNK

    @pl.when(ci == 0)
    def _():
        c_sc[...] = jnp.zeros(c_sc.shape, F32)
        n_sc[...] = jnp.zeros(n_sc.shape, F32)
        m_sc[...] = jnp.zeros(m_sc.shape, F32)
        c_sc[:, 0:dk, :] = c0_ref[...]
        n_sc[:, :, 0:dk] = n0_ref[...]
        m_sc[...] = m0_ref[...]

    def rows(ref):
        x = ref[...].astype(F32)
        if n_valid < L:
            x = jnp.concatenate([x, jnp.zeros((L - n_valid, x.shape[1]), F32)], axis=0)
        return x

    q = rows(q_ref).astype(BF16)
    k = rows(k_ref)
    v = rows(v_ref).astype(BF16)
    og = rows(o_ref)
    graw = rows(g_ref) + bif_ref[...]
    lane = lax.broadcasted_iota(I32, (L, LANES), 1)
    row = lax.broadcasted_iota(I32, (L, LANES), 0)
    is_i = lane < nh
    is_f = (lane >= nh) & (lane < 2 * nh)
    live = row < n_valid
    gi = jnp.where(is_i, jnp.where(live, graw, NEG_INF), 0.0)
    gf = jnp.where(is_f & live, _log_sigmoid(graw), 0.0)
    tril = (lax.broadcasted_iota(I32, (L, L), 0) >= lax.broadcasted_iota(I32, (L, L), 1))
    bcol = jnp.dot(tril.astype(F32), gf, precision=HIGHEST, preferred_element_type=F32)
    brow = bcol.T
    irow = gi.T
    scale = dk ** -0.5

    def heads(x):
        return jnp.stack([x[:, h * LANES:(h + 1) * LANES] for h in range(nh)], axis=0)

    def bdot(a, b, ca, cb):
        return lax.dot_general(a, b, (((ca,), (cb,)), ((0,), (0,))), preferred_element_type=F32)

    q3, v3 = heads(q), heads(v)
    k3 = heads(k)
    k3b = k3.astype(BF16)
    b_c = jnp.stack([bcol[:, nh + h:nh + h + 1] for h in range(nh)], axis=0)
    b_r = jnp.stack([brow[nh + h:nh + h + 1, :] for h in range(nh)], axis=0)
    i_r = jnp.stack([irow[h:h + 1, :] for h in range(nh)], axis=0)
    m_prev = m_sc[...]
    c3 = c_sc[...]
    n3 = n_sc[...]
    dlog = jnp.where(tril[None], b_c - b_r + i_r, NEG_INF)
    m_inter = b_c + m_prev
    m_t = jnp.maximum(m_inter, jnp.max(dlog, axis=-1, keepdims=True))
    sc = bdot(q3, k3b, 2, 2) * scale * jnp.exp(dlog - m_t)
    inter = jnp.exp(m_inter - m_t)
    num = bdot(sc.astype(BF16), v3, 2, 1) + inter * (bdot(q3, c3.astype(BF16), 2, 1) * scale)
    qn = jnp.sum(q3.astype(F32) * n3, axis=-1, keepdims=True) * scale
    den = jnp.sum(sc, axis=-1, keepdims=True) + inter * qn
    hh = num / jnp.maximum(jnp.abs(den), jnp.exp(-m_t))
    b_last = b_c[:, L - 1:L, :]
    g_r = b_last - b_r + i_r
    m_new = jnp.maximum(b_last + m_prev, jnp.max(g_r, axis=-1, keepdims=True))
    ws = jnp.exp(g_r - m_new)
    decay = jnp.exp(b_last + m_prev - m_new)
    kw = (jnp.stack([k3[h].T for h in range(nh)], axis=0) * ws).astype(BF16)
    c_sc[...] = decay * c3 + bdot(kw, v3, 2, 1)
    ws8 = jnp.broadcast_to(ws, (nh, 8, L)).astype(BF16)
    n_sc[...] = decay * n3 + bdot(ws8, k3b, 2, 1)[:, 0:1, :]
    m_sc[...] = m_new
    mu = jnp.mean(hh, axis=-1, keepdims=True)
    var = jnp.mean(jnp.square(hh - mu), axis=-1, keepdims=True)
    out = jax.nn.sigmoid(heads(og)) * ((hh - mu) * lax.rsqrt(var + EPS) * heads(ng_ref[...]))
    for h in range(nh):
        h_ref[:, h * LANES:(h + 1) * LANES] = out[h, 0:n_valid, :].astype(h_ref.dtype)

    @pl.when(ci == pl.num_programs(1) - 1)
    def _():
        c1_ref[...] = c_sc[:, 0:dk, :]
        n1_ref[...] = n_sc[:, :, 0:dk]
        m1_ref[...] = m_sc[...]


def _mlstm(qm, km, vm, om, gates, bif_pad, norm_g, c0, n0, m0, n_valid, out_dtype, name):
    b, s, w = qm.shape
    nh, dk, dv = MLSTM_HEADS, MLSTM_DK, MLSTM_DV
    nc = s // n_valid
    assert s % n_valid == 0 and (n_valid == MLSTM_CHUNK or nc == 1)
    tok = lambda width: pl.BlockSpec((None, n_valid, width), lambda i, c: (i, c, 0))
    const = lambda shape: pl.BlockSpec(shape, lambda i, c: (0,) * len(shape))
    h, c1, n1, m1 = pl.pallas_call(
        functools.partial(_mlstm_kernel, n_valid),
        grid=(b, nc),
        in_specs=[tok(w), tok(w), tok(w), tok(w), tok(LANES), const((1, LANES)), const((1, w)),
                  pl.BlockSpec((None, nh, dk, dv), lambda i, c: (i, 0, 0, 0)),
                  pl.BlockSpec((None, nh, 1, dk), lambda i, c: (i, 0, 0, 0)),
                  pl.BlockSpec((None, nh, 1, 1), lambda i, c: (i, 0, 0, 0))],
        out_specs=[tok(w),
                   pl.BlockSpec((None, nh, dk, dv), lambda i, c: (i, 0, 0, 0)),
                   pl.BlockSpec((None, nh, 1, dk), lambda i, c: (i, 0, 0, 0)),
                   pl.BlockSpec((None, nh, 1, 1), lambda i, c: (i, 0, 0, 0))],
        out_shape=[SDS((b, s, w), out_dtype), SDS((b, nh, dk, dv), F32), SDS((b, nh, 1, dk), F32),
                   SDS((b, nh, 1, 1), F32)],
        scratch_shapes=[pltpu.VMEM((nh, DK_PAD, dv), F32), pltpu.VMEM((nh, 1, DK_PAD), F32),
                        pltpu.VMEM((nh, 1, 1), F32)],
        compiler_params=_cp("parallel", "arbitrary"),
        name=name,
    )(qm, km, vm, om, gates, bif_pad, norm_g, c0, n0.reshape(b, nh, 1, dk), m0.reshape(b, nh, 1, 1))
    return h, c1, n1.reshape(b, nh, dk), m1.reshape(b, nh)


def _pack_bf16_pairs(x):
    w = x.shape[1] // 2
    bits = lax.bitcast_convert_type(x.astype(BF16).astype(F32), U32)
    return (bits[:, :w] >> 16) | (bits[:, w:] & jnp.uint32(0xFFFF0000))


def _unpack_pairs_f32(p):
    return lax.bitcast_convert_type(p << 16, F32), lax.bitcast_convert_type(p & jnp.uint32(0xFFFF0000), F32)


def _unpack_bf16_pairs(p):
    lo, hi = _unpack_pairs_f32(p)
    return lo.astype(BF16), hi.astype(BF16)


def _merge_kernel(x_ref, sc1_ref, sh1_ref, g1_ref, sc2_ref, sh2_ref,
                  o0_ref, o1_ref, o2_ref, l0_ref, l1_ref, l2_ref, mo_ref,
                  wg_ref, wpa_ref, wpm_ref, wout_ref, lng_ref, lnb_ref, x1_ref, h2p_ref, tok_sc):
    x = x_ref[...]
    d = x.shape[1]
    h = (x * (1.0 + sc1_ref[0]) + sh1_ref[0]).astype(BF16)
    g = jnp.dot(h, wg_ref[...], preferred_element_type=F32)

    def tokens(ref):
        dil, rows, _ = ref.shape
        if dil == 1:
            return ref[0]
        c = tok_sc.shape[0]
        for r in range(dil):
            for j in range(c):
                tok_sc[j, pl.ds(r, rows, stride=dil), :] = ref[r, :, j * LANES:(j + 1) * LANES]
        return jnp.concatenate([tok_sc[j] for j in range(c)], axis=1)

    l0, l1, l2 = tokens(l0_ref), tokens(l1_ref), tokens(l2_ref)
    lm = jnp.maximum(jnp.maximum(l0, l1), l2)
    e0, e1, e2 = jnp.exp(l0 - lm), jnp.exp(l1 - lm), jnp.exp(l2 - lm)
    ao = (e0 * tokens(o0_ref) + e1 * tokens(o1_ref) + e2 * tokens(o2_ref)) / (e0 + e1 + e2)
    pa = jnp.dot(ao.astype(BF16), wpa_ref[...], preferred_element_type=F32)
    pm = jnp.dot(mo_ref[...].astype(BF16), wpm_ref[...], preferred_element_type=F32)
    merged = jax.nn.sigmoid(g[:, :d]) * pa + jax.nn.sigmoid(g[:, d:]) * pm
    y = jnp.dot(merged.astype(BF16), wout_ref[...], preferred_element_type=F32)
    x1 = _layer_norm(ALPHA * x + g1_ref[0] * y, lng_ref[...], lnb_ref[...])
    x1_ref[...] = x1
    h2p_ref[...] = _pack_bf16_pairs(x1 * (1.0 + sc2_ref[0]) + sh2_ref[0])


def _merge(x, mods, attn, mo, wg, wpa, wpm, wout, lng, lnb, tm, tiles_per_batch, name):
    n, d = x.shape
    tok = lambda width: pl.BlockSpec((tm, width), lambda i: (i, 0))
    const = lambda a: pl.BlockSpec(a.shape, lambda i: (0,) * a.ndim)

    def lattice(a):
        dil, lat = a.shape[1], a.shape[2]
        per_batch = dil * lat // tm
        return pl.BlockSpec((None, dil, tm // dil, a.shape[3]), lambda i: (i // per_batch, 0, i % per_batch, 0))

    return pl.pallas_call(
        _merge_kernel,
        grid=(n // tm,),
        in_specs=[tok(d)] + [_mod_spec(m, tm, tiles_per_batch) for m in mods]
        + [lattice(a) for a in attn] + [tok(mo.shape[1])]
        + [const(a) for a in (wg, wpa, wpm, wout, lng, lnb)],
        out_specs=[tok(d), tok(d // 2)],
        out_shape=[SDS((n, d), F32), SDS((n, d // 2), U32)],
        scratch_shapes=[pltpu.VMEM((GROUP_WIDTH // LANES, tm, LANES), F32)],
        compiler_params=_cp("parallel"),
        name=name,
    )(x, *mods, *attn, mo, wg, wpa, wpm, wout, lng, lnb)


def _first_index_of_max(x, idx, big):
    mx = jnp.max(x, axis=0, keepdims=True)
    first = jnp.min(jnp.where(x == mx, idx, big), axis=0, keepdims=True)
    return mx, first


def _route_kernel(hp_ref, wlo_ref, whi_ref, bias_ref, idx_ref, w_ref, rank_ref, cnt_ref):
    tm = hp_ref.shape[0]
    ne = N_EXPERTS
    per = ne // N_EXPERT_GROUPS
    lo, hi = _unpack_bf16_pairs(hp_ref[...])
    logits = _nt_dot(wlo_ref[...], lo) + _nt_dot(whi_ref[...], hi)
    scores = jax.nn.sigmoid(logits)
    sel = scores + bias_ref[...]
    erow = lax.broadcasted_iota(I32, (ne, tm), 0).astype(F32)
    prow = lax.broadcasted_iota(I32, (per, tm), 0).astype(F32)
    gs = []
    for g in range(N_EXPERT_GROUPS):
        xg = sel[g * per:(g + 1) * per, :]
        m1, i1 = _first_index_of_max(xg, prow, per)
        m2 = jnp.max(jnp.where(prow == i1, NEG_INF, xg), axis=0, keepdims=True)
        gs.append(m1 + m2)
    gs = jnp.concatenate(gs, axis=0)
    grow = lax.broadcasted_iota(I32, gs.shape, 0).astype(F32)
    gkeep = jnp.zeros(gs.shape, F32)
    for _ in range(TOPK_GROUPS):
        _, gi = _first_index_of_max(gs, grow, N_EXPERT_GROUPS)
        pick = grow == gi
        gkeep = jnp.where(pick, 1.0, gkeep)
        gs = jnp.where(pick, NEG_INF, gs)
    keep = jnp.concatenate([jnp.broadcast_to(gkeep[g:g + 1, :], (per, tm)) for g in range(N_EXPERT_GROUPS)], axis=0)
    cand = jnp.where(keep > 0, sel, NEG_INF)
    member = jnp.zeros((ne, tm), F32)
    picks, idxs, ws = [], [], []
    for _ in range(TOP_K):
        _, ei = _first_index_of_max(cand, erow, ne)
        pick = erow == ei
        picks.append(pick)
        idxs.append(ei)
        ws.append(jnp.sum(jnp.where(pick, scores, 0.0), axis=0, keepdims=True))
        cand = jnp.where(pick, NEG_INF, cand)
        member = jnp.where(pick, 1.0, member)
    wsum = ws[0]
    for wk in ws[1:]:
        wsum = wsum + wk
    idx_ref[...] = jnp.concatenate(idxs, axis=0).astype(I32)
    w_ref[...] = jnp.concatenate(ws, axis=0) / wsum * ROUTED_SCALE
    ti = lax.broadcasted_iota(I32, (tm, tm), 0)
    tj = lax.broadcasted_iota(I32, (tm, tm), 1)
    before = (ti < tj).astype(BF16)
    mb = member.astype(BF16)
    prefix = jnp.dot(mb, before, preferred_element_type=F32)
    rank_ref[...] = jnp.concatenate(
        [jnp.sum(jnp.where(p, prefix, 0.0), axis=0, keepdims=True) for p in picks], axis=0).astype(I32)
    cnt_ref[...] = _nt_dot(jnp.ones((8, tm), BF16), mb).astype(I32)


def _route(h2p, w_router_t, bias_col):
    n, half = h2p.shape
    tm = ROUTE_TILE
    nt = n // tm
    wlo = w_router_t[:, :half]
    whi = w_router_t[:, half:]
    pair = lambda dt: SDS((TOP_K, n), dt)
    return pl.pallas_call(
        _route_kernel,
        grid=(nt,),
        in_specs=[pl.BlockSpec((tm, half), lambda i: (i, 0)),
                  pl.BlockSpec(wlo.shape, lambda i: (0, 0)),
                  pl.BlockSpec(whi.shape, lambda i: (0, 0)),
                  pl.BlockSpec(bias_col.shape, lambda i: (0, 0))],
        out_specs=[pl.BlockSpec((TOP_K, tm), lambda i: (0, i))] * 3
        + [pl.BlockSpec((None, 8, N_EXPERTS), lambda i: (i, 0, 0))],
        out_shape=[pair(I32), pair(F32), pair(I32), SDS((nt, 8, N_EXPERTS), I32)],
        compiler_params=_cp("parallel"),
        name="route",
    )(h2p, wlo, whi, bias_col)


def _dest_kernel(idx_ref, rank_ref, base_ref, dest_ref):
    tm = idx_ref.shape[1]
    erow = lax.broadcasted_iota(I32, (N_EXPERTS, tm), 0)
    base = base_ref[...]
    idx = idx_ref[...]
    rows = [jnp.sum(jnp.where(erow == idx[k:k + 1, :], base, 0.0), axis=0, keepdims=True) for k in range(TOP_K)]
    dest = jnp.concatenate(rows, axis=0).astype(I32) + rank_ref[...]
    tile = dest_ref.shape[2]
    for j in range(dest_ref.shape[0]):
        dest_ref[j] = dest[:, j * tile:(j + 1) * tile]


def _dest(idx, rank, base_cols, tile):
    n = idx.shape[1]
    per_route = ROUTE_TILE // tile
    return pl.pallas_call(
        _dest_kernel,
        grid=(n // ROUTE_TILE,),
        in_specs=[pl.BlockSpec((TOP_K, ROUTE_TILE), lambda i: (0, i)),
                  pl.BlockSpec((TOP_K, ROUTE_TILE), lambda i: (0, i)),
                  pl.BlockSpec((None, N_EXPERTS, 1), lambda i: (i, 0, 0))],
        out_specs=pl.BlockSpec((per_route, TOP_K, tile), lambda i: (i, 0, 0)),
        out_shape=SDS((n // tile, TOP_K, tile), I32),
        compiler_params=_cp("parallel"),
        name="dest",
    )(idx, rank, base_cols)


def _row_copy_loop(tile, make_copy, start):
    def body(t, carry):
        for k in range(TOP_K):
            cp = make_copy(k, t)
            if start:
                cp.start(priority=k % 2)
            else:
                cp.wait()
        return carry
    lax.fori_loop(0, tile, body, 0, unroll=8)


def _dest_fetch(dest_hbm, tile_idx, dest_smem, sem_idx):
    n_idx = dest_smem.shape[0]
    start = tile_idx * n_idx
    if not isinstance(start, int):
        start = pl.multiple_of(start, n_idx)
    return pltpu.make_async_copy(dest_hbm.at[pl.ds(start, n_idx)], dest_smem, sem_idx)


def _rows_to_pieces(x, ref):
    c = x.shape[1] // LANES
    for j in range(c):
        ref[pl.ds(j, x.shape[0], stride=c), :] = x[:, j * LANES:(j + 1) * LANES]


def _pieces_to_rows(ref, rows, c):
    return jnp.concatenate([ref[pl.ds(j, rows, stride=c), :] for j in range(c)], axis=1)


def _piece(ref, row, c):
    return ref.at[pl.ds(pl.multiple_of(row * c, c), c)]


def _scatter_kernel(pends_ref, dest_hbm, hp_ref, xs_out, stage, zeros_sc, dest_smem, sem_idx, sem_rows, sem_zero):
    i = pl.program_id(0)
    n = pl.num_programs(0)
    tile = hp_ref.shape[0]
    c = hp_ref.shape[1] // LANES
    slot = lax.rem(i, 2)
    idx_cp = _dest_fetch(dest_hbm, i, dest_smem, sem_idx)
    idx_cp.start()

    @pl.when(i == 0)
    def _():
        blk = zeros_sc.shape[0]
        zeros_sc[...] = jnp.zeros(zeros_sc.shape, zeros_sc.dtype)

        def zero_copy(e):
            start = jnp.maximum(pends_ref[e] * c - blk, 0)
            return pltpu.make_async_copy(zeros_sc, xs_out.at[pl.ds(pl.multiple_of(start, blk), blk)], sem_zero)

        def start_body(e, carry):
            zero_copy(e).start()
            return carry

        def wait_body(e, carry):
            zero_copy(e).wait()
            return carry

        lax.fori_loop(0, N_EXPERTS, start_body, 0)
        lax.fori_loop(0, N_EXPERTS, wait_body, 0)

    def row_copy(k, t):
        return pltpu.make_async_copy(_piece(stage.at[slot], t, c), _piece(xs_out, dest_smem[k * tile + t], c),
                                     sem_rows.at[slot])

    def wait_tile(s):
        _row_copy_loop(tile, lambda k, t: pltpu.make_async_copy(
            stage.at[s, pl.ds(0, c)], xs_out.at[pl.ds(0, c)], sem_rows.at[s]), False)

    @pl.when(i >= 2)
    def _():
        wait_tile(slot)

    _rows_to_pieces(hp_ref[...], stage.at[slot])
    idx_cp.wait()
    _row_copy_loop(tile, row_copy, True)

    @pl.when(i == n - 1)
    def _():
        @pl.when(n >= 2)
        def _():
            wait_tile(1 - slot)
        wait_tile(slot)


def _scatter_rows(pends, dest_flat, h2p, n_rows):
    n, half = h2p.shape
    tile = MOE_TILE
    c = half // LANES
    grid_spec = pltpu.PrefetchScalarGridSpec(
        num_scalar_prefetch=1,
        grid=(n // tile,),
        in_specs=[pl.BlockSpec(memory_space=pl.ANY),
                  pl.BlockSpec((tile, half), lambda i, pe: (i, 0))],
        out_specs=pl.BlockSpec(memory_space=pl.ANY),
        scratch_shapes=[pltpu.VMEM((2, tile * c, LANES), U32), pltpu.VMEM((MOE_BLOCK * c, LANES), U32),
                        pltpu.SMEM((TOP_K * tile,), I32),
                        pltpu.SemaphoreType.DMA, pltpu.SemaphoreType.DMA((2,)), pltpu.SemaphoreType.DMA],
    )
    return pl.pallas_call(
        _scatter_kernel,
        grid_spec=grid_spec,
        out_shape=SDS((n_rows * c, LANES), U32),
        compiler_params=_cp("arbitrary"),
        name="scatter_rows",
    )(pends, dest_flat, h2p)


def _expert_kernel(blk_exp_ref, n_used_ref, xs_ref, win_ref, wout_ref, ys_ref, win_sc, wout_sc):
    i = pl.program_id(0)
    prev = blk_exp_ref[jnp.maximum(i - 1, 0)]

    @pl.when((i == 0) | (blk_exp_ref[i] != prev))
    def _():
        win_sc[...] = win_ref[...].astype(BF16)
        wout_sc[...] = wout_ref[...].astype(BF16)

    @pl.when(i < n_used_ref[0])
    def _():
        half = win_sc.shape[0] // 2
        lo, hi = _unpack_bf16_pairs(_pieces_to_rows(xs_ref, MOE_BLOCK, half // LANES))
        ag = (jnp.dot(lo, win_sc[0:half, :], preferred_element_type=F32)
              + jnp.dot(hi, win_sc[half:2 * half, :], preferred_element_type=F32))
        a = ag[:, :D_EXPERT]
        g = ag[:, D_EXPERT:]
        mid = (a * jax.nn.sigmoid(a) * g).astype(BF16)
        _rows_to_pieces(_pack_bf16_pairs(jnp.dot(mid, wout_sc[...], preferred_element_type=F32)), ys_ref)

    @pl.when(i >= n_used_ref[0])
    def _():
        ys_ref[...] = jnp.zeros(ys_ref.shape, ys_ref.dtype)


def _experts(xs, blk_exp, n_used, w_exp_in, w_exp_out):
    d = w_exp_in.shape[1]
    cx = cy = d // 2 // LANES
    n_rows = xs.shape[0] // cx
    nb = n_rows // MOE_BLOCK
    grid_spec = pltpu.PrefetchScalarGridSpec(
        num_scalar_prefetch=2,
        grid=(nb,),
        in_specs=[pl.BlockSpec((MOE_BLOCK * cx, LANES), lambda i, be, nu: (jnp.where(i < nu[0], i, 0), 0)),
                  pl.BlockSpec((None, d, 2 * D_EXPERT), lambda i, be, nu: (be[i], 0, 0)),
                  pl.BlockSpec((None, D_EXPERT, d), lambda i, be, nu: (be[i], 0, 0))],
        out_specs=pl.BlockSpec((MOE_BLOCK * cy, LANES), lambda i, be, nu: (i, 0)),
        scratch_shapes=[pltpu.VMEM((d, 2 * D_EXPERT), BF16), pltpu.VMEM((D_EXPERT, d), BF16)],
    )
    return pl.pallas_call(
        _expert_kernel,
        grid_spec=grid_spec,
        out_shape=SDS((n_rows * cy, LANES), U32),
        compiler_params=_cp("arbitrary"),
        name="experts",
    )(blk_exp, n_used, xs, w_exp_in, w_exp_out)


def _final_kernel(dest_hbm, x1_ref, hp_ref, w_ref, g2_ref, ys_hbm, wsin_ref, wsout_ref, lng_ref, lnb_ref,
                  y_ref, rows_sc, dest_smem, sem_idx, sem_rows, *, tile_offset):
    i = pl.program_id(0)
    n = pl.num_programs(0)
    tile = x1_ref.shape[0]
    c = hp_ref.shape[1] // LANES
    slot = lax.rem(i, 2)

    def start_rows(s):
        _row_copy_loop(tile, lambda k, t: pltpu.make_async_copy(
            _piece(ys_hbm, dest_smem[k * tile + t], c), _piece(rows_sc.at[s, k], t, c), sem_rows.at[s]), True)

    @pl.when(i == 0)
    def _():
        first = _dest_fetch(dest_hbm, tile_offset, dest_smem, sem_idx)
        first.start()
        first.wait()
        start_rows(0)

    nxt = _dest_fetch(dest_hbm, jnp.minimum(i + 1, n - 1) + tile_offset, dest_smem, sem_idx)

    @pl.when(i + 1 < n)
    def _():
        nxt.start()

    half = hp_ref.shape[1]
    lo, hi = _unpack_bf16_pairs(hp_ref[...])
    ag = (jnp.dot(lo, wsin_ref[0:half, :], preferred_element_type=F32)
          + jnp.dot(hi, wsin_ref[half:2 * half, :], preferred_element_type=F32))
    a = ag[:, :D_SHARED]
    g = ag[:, D_SHARED:]
    y2 = jnp.dot((a * jax.nn.sigmoid(a) * g).astype(BF16), wsout_ref[...], preferred_element_type=F32)
    w_cols = jnp.concatenate([w_ref[...], jnp.zeros((tile - TOP_K, tile), F32)], axis=0).T

    @pl.when(i + 1 < n)
    def _():
        nxt.wait()
        start_rows(1 - slot)

    _row_copy_loop(tile, lambda k, t: pltpu.make_async_copy(
        ys_hbm.at[pl.ds(0, c)], rows_sc.at[slot, 0, pl.ds(0, c)], sem_rows.at[slot]), False)
    y_lo, y_hi = y2[:, :half], y2[:, half:]
    for k in range(TOP_K):
        r_lo, r_hi = _unpack_pairs_f32(_pieces_to_rows(rows_sc.at[slot, k], tile, c))
        y_lo = y_lo + w_cols[:, k:k + 1] * r_lo
        y_hi = y_hi + w_cols[:, k:k + 1] * r_hi
    y2 = jnp.concatenate([y_lo, y_hi], axis=1)
    y_ref[...] = _layer_norm(ALPHA * x1_ref[...] + g2_ref[0] * y2, lng_ref[...], lnb_ref[...])


def _final(dest_flat, x1, h2p, w_top, g2_3, ys, wsin, wsout, lng, lnb, tile_offset, tiles_per_batch, name):
    n, d = x1.shape
    tile = MOE_TILE
    assert tile == LANES
    const = lambda a: pl.BlockSpec(a.shape, lambda i: (0,) * a.ndim)
    return pl.pallas_call(
        functools.partial(_final_kernel, tile_offset=tile_offset),
        grid=(n // tile,),
        in_specs=[pl.BlockSpec(memory_space=pl.ANY),
                  pl.BlockSpec((tile, d), lambda i: (i, 0)),
                  pl.BlockSpec((tile, d // 2), lambda i: (i, 0)),
                  pl.BlockSpec((TOP_K, tile), lambda i: (0, i)),
                  _mod_spec(g2_3, tile, tiles_per_batch),
                  pl.BlockSpec(memory_space=pl.ANY),
                  const(wsin), const(wsout), const(lng), const(lnb)],
        out_specs=pl.BlockSpec((tile, d), lambda i: (i, 0)),
        out_shape=SDS((n, d), F32),
        scratch_shapes=[pltpu.VMEM((2, TOP_K, tile * (d // 2 // LANES), LANES), U32), pltpu.SMEM((TOP_K * tile,), I32),
                        pltpu.SemaphoreType.DMA, pltpu.SemaphoreType.DMA((2,))],
        compiler_params=_cp("arbitrary"),
        name=name,
    )(dest_flat, x1, h2p, w_top, g2_3, ys, wsin, wsout, lng, lnb)


def _sorted_layout(counts):
    totals = jnp.sum(counts, axis=0)
    padded = (totals + MOE_BLOCK - 1) // MOE_BLOCK * MOE_BLOCK
    pends = jnp.cumsum(padded)
    base = (pends - padded)[None, :] + jnp.cumsum(counts, axis=0) - counts
    return base.astype(I32), pends


def _moe_routed(h2p, w_router, router_bias, w_exp_in, w_exp_out):
    n = h2p.shape[0]
    idx, w_top, rank, cnt = _route(h2p, w_router.T.astype(BF16), router_bias.reshape(N_EXPERTS, 1))
    base, pends = _sorted_layout(cnt[:, 0, :])
    n_blocks = -(-(n * TOP_K + N_EXPERTS * (MOE_BLOCK - 1)) // MOE_BLOCK)
    blk_start = jnp.arange(n_blocks, dtype=I32) * MOE_BLOCK
    blk_exp = jnp.minimum(jnp.sum((pends[None, :] <= blk_start[:, None]).astype(I32), axis=1), N_EXPERTS - 1)
    n_used = (pends[-1:] // MOE_BLOCK).astype(I32)
    dest = _dest(idx, rank, base.astype(F32)[:, :, None], MOE_TILE).reshape(-1)
    xs = _scatter_rows(pends.astype(I32), dest, h2p, n_blocks * MOE_BLOCK)
    ys = _experts(xs, blk_exp, n_used, w_exp_in, w_exp_out)
    return dest, w_top, ys


def _pad_heads(w, nh, dk):
    d = w.shape[0]
    return jnp.pad(w.reshape(d, nh, dk), ((0, 0), (0, 0), (0, DK_PAD - dk))).reshape(d, nh * DK_PAD)


def _split_w_in(w_in):
    aw = N_GROUPS * GROUP_WIDTH
    nh, dk, dv = MLSTM_HEADS, MLSTM_DK, MLSTM_DV
    offs = np.cumsum([0, aw, aw, aw, nh * dk, nh * dk, nh * dv, nh * dv, nh, nh, w_in.shape[0], w_in.shape[0]])
    seg = [w_in[:, offs[i]:offs[i + 1]] for i in range(11)]
    q_a, k_a, v_a, q_m, k_m, v_m, o_m, i_m, f_m, g_a, g_b = seg
    bf = lambda a: a.astype(BF16)
    w_gates = jnp.pad(jnp.concatenate([i_m, f_m], axis=1), ((0, 0), (0, LANES - 2 * nh)))
    w_gates_hi = w_gates.astype(BF16)
    w_gates = jnp.concatenate([w_gates_hi, (w_gates - w_gates_hi.astype(F32)).astype(BF16)], axis=1)
    return dict(qa=bf(q_a), kva=bf(jnp.concatenate([k_a, v_a], axis=1)),
                qm=bf(_pad_heads(q_m, nh, dk)), km=bf(_pad_heads(k_m, nh, dk)), vm=bf(v_m), om=bf(o_m),
                gates=w_gates, g=bf(jnp.concatenate([g_a, g_b], axis=1)))


def _mixing(x, mods, tm, tiles_per_batch, batch, seq, caches, states, wts, prompt):
    n, d = x.shape
    sc1, sh1, g1 = mods["scale1"], mods["shift1"], mods["gate1"]
    sc2, sh2 = mods["scale2"], mods["shift2"]
    act = BF16 if prompt else F32
    if prompt:
        tail = min(max(w for w, _ in ATTN_GROUPS), seq)
        qkv_l, kv_tail = _proj_lattice(x, sc1, sh1, wts["qa"], wts["kva"], batch, seq, tm, tail)
    else:
        q_a, kv_a = _proj(x, sc1, sh1, [wts["qa"], wts["kva"]], [F32, F32], None, tm, tiles_per_batch, "proj_attn_s")
    qm, km, vm, om, gates = _proj(x, sc1, sh1, [wts["qm"], wts["km"], wts["vm"], wts["om"]],
                                  [act, act, act, F32], wts["gates"], tm, tiles_per_batch,
                                  "proj_mlstm_p" if prompt else "proj_mlstm_s")
    attn_o, attn_l, bufs = [], [], []
    for g, (window, dilation) in enumerate(ATTN_GROUPS):
        if prompt:
            o, l = _band_attn(*qkv_l[g], window)
            keep = min(window, seq)
            kv3 = kv_tail.reshape(batch, tail, 2, N_GROUPS, GROUP_WIDTH)[:, tail - keep:, :, g, :]
            bufs.append(kv3.reshape(1, batch, keep, 2, ATTN_HEADS, ATTN_HEAD_DIM))
        else:
            o, l, nc = _cache_attn(q_a.reshape(batch, seq, -1), kv_a.reshape(batch, seq, -1), caches[g], g,
                                   window, dilation)
            o, l = o[None, None], l[None, None]
            bufs.append(nc[None])
        attn_o.append(o)
        attn_l.append(l)
    r3 = lambda a: a.reshape(batch, seq, a.shape[-1])
    n_valid = MLSTM_CHUNK if prompt else seq
    mo, c1, n1, m1 = _mlstm(r3(qm), r3(km), r3(vm), r3(om), r3(gates), wts["bif"], wts["norm_g"],
                            states[0], states[1], states[2], n_valid, act,
                            "mlstm_p" if prompt else "mlstm_s")
    x1, h2p = _merge(x, [sc1, sh1, g1, sc2, sh2], attn_o + attn_l, mo.reshape(n, -1),
                     wts["g"], wts["pa"], wts["pm"], wts["out"], wts["ln1_g"], wts["ln1_b"],
                     tm, tiles_per_batch, "merge_p" if prompt else "merge_s")
    return x1, h2p, bufs, (c1[None], n1[None], m1[None])


def _mod_pieces(mod, d, rows_per_batch, tm):
    names = ("shift1", "scale1", "gate1", "shift2", "scale2", "gate2")
    out = {}
    for p, name in enumerate(names):
        piece = mod[:, p * d:(p + 1) * d]
        if rows_per_batch % tm == 0:
            out[name] = piece[:, None, :]
        else:
            out[name] = jnp.repeat(piece, rows_per_batch, axis=0).reshape(-1, tm, d)
    return out


def kernel(x_prompt, x_sample, cache_kv_w128, cache_kv_w512, cache_kv_w2048, state_mlstm_C, state_mlstm_n, state_mlstm_m, c_prompt, c_sample, w_ada, b_ada, w_in, b_if, mlstm_norm_g, w_proj_attn, w_proj_mlstm, w_out, ln1_g, ln1_b, w_router, router_bias, w_exp_in, w_exp_out, w_sh_in, w_sh_out, ln2_g, ln2_b):
    assert w_ada.shape[0] == DEPTH
    bp, sp, d = x_prompt.shape
    bs, ss, _ = x_sample.shape
    nh = MLSTM_HEADS
    np_, ns = bp * sp, bs * ss

    mod = _ada(jnp.concatenate([c_prompt, c_sample], axis=0), w_ada[0], b_ada[0])
    wts = _split_w_in(w_in[0])
    wts.update(
        bif=jnp.pad(b_if[0], (0, LANES - 2 * nh)).reshape(1, LANES),
        norm_g=mlstm_norm_g[0].reshape(1, -1),
        pa=w_proj_attn[0].astype(BF16), pm=w_proj_mlstm[0].astype(BF16), out=w_out[0].astype(BF16),
        ln1_g=ln1_g[0].reshape(1, d), ln1_b=ln1_b[0].reshape(1, d))

    tm_p, tm_s = 512, 256
    mods_p = _mod_pieces(mod[:bp], d, sp, tm_p)
    mods_s = _mod_pieces(mod[bp:], d, ss, tm_s)
    zeros_p = (jnp.zeros((bp, nh, MLSTM_DK, MLSTM_DV), F32), jnp.zeros((bp, nh, MLSTM_DK), F32),
               jnp.zeros((bp, nh), F32))
    x1p, h2p_p, bufs_p, st_p = _mixing(x_prompt.reshape(np_, d), mods_p, tm_p, sp // tm_p, bp, sp, None,
                                       zeros_p, wts, True)
    caches = (cache_kv_w128[0], cache_kv_w512[0], cache_kv_w2048[0])
    states = (state_mlstm_C[0], state_mlstm_n[0], state_mlstm_m[0])
    x1s, h2p_s, bufs_s, st_s = _mixing(x_sample.reshape(ns, d), mods_s, tm_s, 1, bs, ss, caches, states, wts, False)

    h2p = jnp.concatenate([h2p_p, h2p_s], axis=0)
    dest, w_top, ys = _moe_routed(h2p, w_router[0], router_bias[0], w_exp_in[0], w_exp_out[0])
    wsin, wsout = w_sh_in[0].astype(BF16), w_sh_out[0].astype(BF16)
    lng, lnb = ln2_g[0].reshape(1, d), ln2_b[0].reshape(1, d)
    g2_p = _mod_pieces(mod[:bp], d, sp, MOE_TILE)["gate2"]
    g2_s = _mod_pieces(mod[bp:], d, ss, MOE_TILE)["gate2"]
    y_p = _final(dest, x1p, h2p_p, w_top[:, :np_], g2_p, ys, wsin, wsout, lng, lnb, 0, sp // MOE_TILE, "final_p")
    y_s = _final(dest, x1s, h2p_s, w_top[:, np_:], g2_s, ys, wsin, wsout, lng, lnb, np_ // MOE_TILE, 1, "final_s")

    return (y_p.reshape(bp, sp, d), y_s.reshape(bs, ss, d),
            bufs_p[0], bufs_p[1], bufs_p[2], st_p[0], st_p[1], st_p[2],
            bufs_s[0], bufs_s[1], bufs_s[2], st_s[0], st_s[1], st_s[2])
```

```python
import functools

import numpy as np
import jax
import jax.numpy as jnp
from jax import lax
from jax.experimental import pallas as pl
from jax.experimental.pallas import tpu as pltpu

F32 = jnp.float32
BF16 = jnp.bfloat16
I32 = jnp.int32
U32 = jnp.uint32
HIGHEST = lax.Precision.HIGHEST
SDS = jax.ShapeDtypeStruct
NEG_INF = float("-inf")

ATTN_GROUPS = ((128, 1), (512, 4), (2048, 16))
N_GROUPS = 3
ATTN_HEADS = 4
ATTN_HEAD_DIM = 64
GROUP_WIDTH = ATTN_HEADS * ATTN_HEAD_DIM
ATTN_BLOCK = 128
ATTN_STEP_BLOCKS = 2
MLSTM_HEADS = 8
MLSTM_DK = 64
MLSTM_DV = 128
N_EXPERTS = 256
TOP_K = 8
N_EXPERT_GROUPS = 8
TOPK_GROUPS = 4
D_EXPERT = 256
D_SHARED = 256
ROUTED_SCALE = 2.5
DEPTH = 1
ALPHA = (2 * DEPTH) ** 0.25
EPS = 1e-5

LANES = 128
MLSTM_CHUNK = 128
MLSTM_STEP_CHUNKS = 2
DK_PAD = LANES
MOE_TILE = 128
MOE_BLOCK = 512
ROUTE_TILE = 512
VMEM_LIMIT = 56 << 20


def _cp(*sem, vmem=VMEM_LIMIT):
    return pltpu.CompilerParams(dimension_semantics=sem, vmem_limit_bytes=vmem)


def _nt_dot(a, b):
    return lax.dot_general(a, b, (((1,), (1,)), ((), ())), preferred_element_type=F32)


def _layer_norm(r, g, b):
    mu = jnp.mean(r, axis=-1, keepdims=True)
    var = jnp.mean(jnp.square(r - mu), axis=-1, keepdims=True)
    return (r - mu) * lax.rsqrt(var + EPS) * g + b


def _mod_spec(mod3, tm, tiles_per_batch):
    d = mod3.shape[-1]
    if mod3.shape[1] == 1:
        return pl.BlockSpec((1, 1, d), lambda i: (i // tiles_per_batch, 0, 0))
    return pl.BlockSpec((1, tm, d), lambda i: (i, 0, 0))


def _ada_kernel(c_ref, w_ref, b_ref, o_ref):
    c = c_ref[...]
    s = c * jax.nn.sigmoid(c)
    o_ref[...] = jnp.dot(s, w_ref[...], precision=HIGHEST, preferred_element_type=F32) + b_ref[...]


def _ada(c_all, w_ada, b_ada):
    r, d = c_all.shape
    n = w_ada.shape[1]
    tn = 1024
    return pl.pallas_call(
        _ada_kernel,
        grid=(n // tn,),
        in_specs=[pl.BlockSpec((r, d), lambda j: (0, 0)),
                  pl.BlockSpec((d, tn), lambda j: (0, j)),
                  pl.BlockSpec((1, tn), lambda j: (0, j))],
        out_specs=pl.BlockSpec((r, tn), lambda j: (0, j)),
        out_shape=SDS((r, n), F32),
        compiler_params=_cp("parallel"),
        name="ada",
    )(c_all, w_ada, b_ada.reshape(1, n))


def _proj_kernel(n_w, has_hp, x_ref, sc_ref, sh_ref, *refs):
    w_refs = refs[:n_w]
    pos = n_w
    if has_hp:
        whp_ref = refs[pos]
        pos += 1
    o_refs = refs[pos:pos + n_w]
    h = x_ref[...] * (1.0 + sc_ref[0]) + sh_ref[0]
    hb = h.astype(BF16)
    for w_ref, o_ref in zip(w_refs, o_refs):
        o_ref[...] = jnp.dot(hb, w_ref[...], preferred_element_type=F32).astype(o_ref.dtype)
    if has_hp:
        ohp_ref = refs[pos + n_w]
        wn = ohp_ref.shape[1]
        h_lo = (h - hb.astype(F32)).astype(BF16)
        r = jnp.dot(hb, whp_ref[...], preferred_element_type=F32)
        ohp_ref[...] = (r[:, :wn] + r[:, wn:]) + jnp.dot(h_lo, whp_ref[:, 0:wn], preferred_element_type=F32)


def _proj(x, sc3, sh3, weights, out_dtypes, w_hp, tm, tiles_per_batch, name):
    n, d = x.shape
    n_w = len(weights)
    has_hp = w_hp is not None
    in_specs = [pl.BlockSpec((tm, d), lambda i: (i, 0)),
                _mod_spec(sc3, tm, tiles_per_batch), _mod_spec(sh3, tm, tiles_per_batch)]
    in_specs += [pl.BlockSpec(w.shape, lambda i: (0, 0)) for w in weights]
    out_specs = [pl.BlockSpec((tm, w.shape[1]), lambda i: (i, 0)) for w in weights]
    out_shape = [SDS((n, w.shape[1]), dt) for w, dt in zip(weights, out_dtypes)]
    args = [x, sc3, sh3, *weights]
    if has_hp:
        in_specs.append(pl.BlockSpec(w_hp.shape, lambda i: (0, 0)))
        out_specs.append(pl.BlockSpec((tm, w_hp.shape[1] // 2), lambda i: (i, 0)))
        out_shape.append(SDS((n, w_hp.shape[1] // 2), F32))
        args.append(w_hp)
    return pl.pallas_call(
        functools.partial(_proj_kernel, n_w, has_hp),
        grid=(n // tm,),
        in_specs=in_specs, out_specs=out_specs, out_shape=out_shape,
        compiler_params=_cp("parallel"),
        name=name,
    )(*args)


def _head_masks(width):
    lane_head = lax.broadcasted_iota(I32, (1, width), 1) // ATTN_HEAD_DIM
    return [(lane_head == h).astype(F32) for h in range(ATTN_HEADS)]


def _band_attn_kernel(wband, q_ref, kc_ref, kp_ref, vc_ref, vp_ref, o_ref, l_ref):
    j = pl.program_id(2)
    qb = kp_ref.shape[0]
    nsub = q_ref.shape[0] // qb
    k_all = jnp.concatenate([kp_ref[...], kc_ref[...]], axis=0).astype(BF16)
    v_all = jnp.concatenate([vp_ref[...], vc_ref[...]], axis=0).astype(BF16)
    qi = lax.broadcasted_iota(I32, (qb, 2 * qb), 0)
    kj = lax.broadcasted_iota(I32, (qb, 2 * qb), 1)
    dist = qi + qb - kj
    band = jnp.where(dist >= 0, 1, 0) * jnp.where(dist <= wband, 1, 0)
    hms = _head_masks(q_ref.shape[1])
    for u in range(nsub):
        rows = slice(u * qb, (u + 1) * qb)
        q = q_ref[rows, :].astype(F32)
        k = k_all[u * qb:(u + 2) * qb]
        v = v_all[u * qb:(u + 2) * qb]
        valid = band * jnp.where((j * nsub + u) * qb + kj - qb >= 0, 1, 0)
        bias = jnp.where(valid > 0, 0.0, NEG_INF)
        qs = jnp.concatenate([q * (hm * ATTN_HEAD_DIM ** -0.5) for hm in hms], axis=0).astype(BF16)
        s = _nt_dot(qs, k) + jnp.concatenate([bias] * ATTN_HEADS, axis=0)
        mx = jnp.max(s, axis=-1, keepdims=True)
        p = jnp.exp(s - mx)
        den = jnp.sum(p, axis=-1, keepdims=True)
        o4 = jnp.dot((p / den).astype(BF16), v, preferred_element_type=F32)
        l4 = mx + jnp.log(den)
        o_acc = jnp.zeros((qb, o_ref.shape[1]), F32)
        l_acc = jnp.zeros((qb, l_ref.shape[1]), F32)
        for h, hm in enumerate(hms):
            o_acc = o_acc + o4[h * qb:(h + 1) * qb, :] * hm
            l_acc = l_acc + l4[h * qb:(h + 1) * qb, :] * hm
        o_ref[rows, :] = o_acc
        l_ref[rows, :] = l_acc


def _band_attn(q_l, k_l, v_l, window):
    batch, dilation, lat, gw = q_l.shape
    nsub = ATTN_STEP_BLOCKS if lat % (ATTN_STEP_BLOCKS * ATTN_BLOCK) == 0 else 1
    nb = lat // (nsub * ATTN_BLOCK)
    blk = (None, None, nsub * ATTN_BLOCK, gw)
    blk_prev = (None, None, ATTN_BLOCK, gw)
    cur = lambda b, r, j: (b, r, j, 0)
    prev = lambda b, r, j: (b, r, jnp.maximum(nsub * j - 1, 0), 0)
    return pl.pallas_call(
        functools.partial(_band_attn_kernel, window // dilation),
        grid=(batch, dilation, nb),
        in_specs=[pl.BlockSpec(blk, cur), pl.BlockSpec(blk, cur), pl.BlockSpec(blk_prev, prev),
                  pl.BlockSpec(blk, cur), pl.BlockSpec(blk_prev, prev)],
        out_specs=[pl.BlockSpec(blk, cur), pl.BlockSpec(blk, cur)],
        out_shape=[SDS(q_l.shape, F32)] * 2,
        compiler_params=_cp("parallel", "parallel", "arbitrary"),
        name=f"band_attn_d{dilation}",
    )(q_l, k_l, k_l, v_l, v_l)


def _rows_to_lane_tiles(x, ref):
    for j in range(x.shape[1] // LANES):
        ref[j] = x[:, j * LANES:(j + 1) * LANES]


def _proj_lattice_kernel(dils, t0, x_ref, sc_ref, sh_ref, wq_ref, wkv_ref, *refs):
    ng = len(dils)
    lat_refs = refs[:3 * ng]
    kv_tail_ref = refs[3 * ng]
    zs = refs[3 * ng + 1]
    gw = GROUP_WIDTH
    tm = x_ref.shape[0]
    hb = (x_ref[...] * (1.0 + sc_ref[0]) + sh_ref[0]).astype(BF16)
    zq = jnp.dot(hb, wq_ref[...], preferred_element_type=F32)
    zkv = jnp.dot(hb, wkv_ref[...], preferred_element_type=F32)

    @pl.when(pl.program_id(1) >= t0)
    def _():
        kv_tail_ref[...] = zkv

    for g, d in enumerate(dils):
        for part, z in enumerate((zq[:, g * gw:(g + 1) * gw], zkv[:, g * gw:(g + 1) * gw],
                                  zkv[:, (ng + g) * gw:(ng + g + 1) * gw])):
            out = lat_refs[3 * g + part]
            if d == 1:
                out[0] = z.astype(out.dtype)
            else:
                _rows_to_lane_tiles(z, zs)
                for r in range(d):
                    out[r] = jnp.concatenate([zs[j, pl.ds(r, tm // d, stride=d), :] for j in range(gw // LANES)],
                                             axis=1).astype(out.dtype)


def _proj_lattice(x, sc3, sh3, wq, wkv, batch, seq, tm, tail):
    d_model = x.shape[1]
    tiles = seq // tm
    t0 = (seq - tail) // tm
    assert seq % tm == 0 and (seq - tail) % tm == 0
    dils = tuple(d for _, d in ATTN_GROUPS)
    gw = GROUP_WIDTH
    lat_specs, lat_shapes = [], []
    for d in dils:
        for _ in range(3):
            lat_specs.append(pl.BlockSpec((None, d, tm // d, gw), lambda b, t: (b, 0, t, 0)))
            lat_shapes.append(SDS((batch, d, seq // d, gw), BF16))
    outs = pl.pallas_call(
        functools.partial(_proj_lattice_kernel, dils, t0),
        grid=(batch, tiles),
        in_specs=[pl.BlockSpec((tm, d_model), lambda b, t: (b * tiles + t, 0)),
                  pl.BlockSpec((1, 1, d_model), lambda b, t: (b, 0, 0)),
                  pl.BlockSpec((1, 1, d_model), lambda b, t: (b, 0, 0)),
                  pl.BlockSpec(wq.shape, lambda b, t: (0, 0)),
                  pl.BlockSpec(wkv.shape, lambda b, t: (0, 0))],
        out_specs=lat_specs + [pl.BlockSpec((None, tm, wkv.shape[1]), lambda b, t: (b, jnp.maximum(t - t0, 0), 0))],
        out_shape=lat_shapes + [SDS((batch, tail, wkv.shape[1]), F32)],
        scratch_shapes=[pltpu.VMEM((gw // LANES, tm, LANES), F32)],
        compiler_params=_cp("parallel", "arbitrary"),
        name="proj_attn_p",
    )(x, sc3, sh3, wq, wkv)
    return [outs[3 * g:3 * g + 3] for g in range(len(dils))], outs[-1]


def _cache_attn_kernel(wband, dilation, c_ref, q_ref, kv_ref, o_ref, l_ref, nc_ref):
    lw = c_ref.shape[-1]
    t_new = q_ref.shape[1]
    dh = q_ref.shape[2]

    def band_bias(n_keys, key_base):
        t = lax.broadcasted_iota(I32, (t_new, n_keys), 0)
        p = lax.broadcasted_iota(I32, (t_new, n_keys), 1) + key_base
        delta = lw + t - p
        ok = (jnp.where(delta >= 0, 1, 0) * jnp.where((delta & (dilation - 1)) == 0, 1, 0)
              * jnp.where(delta <= wband * dilation, 1, 0))
        return jnp.where(ok > 0, 0.0, NEG_INF)

    bias_c = band_bias(lw, 0)
    bias_n = band_bias(t_new, lw)
    place = (lax.broadcasted_iota(I32, (t_new, LANES), 1)
             == lax.broadcasted_iota(I32, (t_new, LANES), 0) + (LANES - t_new)).astype(F32)
    tail = lax.broadcasted_iota(I32, (1, LANES), 1) >= LANES - t_new
    for h in range(ATTN_HEADS):
        qh = (q_ref[h] * dh ** -0.5).astype(BF16)
        kt, vt = c_ref[0, h], c_ref[1, h]
        knt, vnt = kv_ref[0, h], kv_ref[1, h]
        sc = jnp.dot(qh, kt.astype(BF16), preferred_element_type=F32) + bias_c
        sn = jnp.dot(qh, knt.astype(BF16), preferred_element_type=F32) + bias_n
        mx = jnp.maximum(jnp.max(sc, axis=-1, keepdims=True), jnp.max(sn, axis=-1, keepdims=True))
        pc = jnp.exp(sc - mx)
        pn = jnp.exp(sn - mx)
        den = jnp.sum(pc, axis=-1, keepdims=True) + jnp.sum(pn, axis=-1, keepdims=True)
        o_ref[h] = (_nt_dot((pc / den).astype(BF16), vt.astype(BF16))
                    + _nt_dot((pn / den).astype(BF16), vnt.astype(BF16)))
        l_ref[h] = jnp.broadcast_to(mx + jnp.log(den), (t_new, dh))
        for kv, (old, new) in enumerate(((kt, knt), (vt, vnt))):
            rolled = pltpu.roll(old, lw - t_new, axis=1)
            new_tile = jnp.dot(new, place, precision=HIGHEST, preferred_element_type=F32)
            if lw > LANES:
                nc_ref[kv, h, :, 0:lw - LANES] = rolled[:, 0:lw - LANES]
            nc_ref[kv, h, :, lw - LANES:lw] = jnp.where(tail, new_tile, rolled[:, lw - LANES:lw])


def _cache_attn(q_s, kv_s, cache, g, window, dilation):
    b, lw = cache.shape[0], cache.shape[1]
    t_new = q_s.shape[1]
    nh, dh = ATTN_HEADS, ATTN_HEAD_DIM
    assert lw == window and lw % LANES == 0 and t_new % 8 == 0 and dilation & (dilation - 1) == 0
    ct = jnp.transpose(cache, (0, 2, 3, 4, 1))
    qh = jnp.transpose(q_s.reshape(b, t_new, N_GROUPS, nh, dh)[:, :, g], (0, 2, 1, 3))
    kvt = jnp.transpose(kv_s.reshape(b, t_new, 2, N_GROUPS, nh, dh)[:, :, :, g], (0, 2, 3, 4, 1))
    win = pl.BlockSpec((None, 2, nh, dh, lw), lambda i: (i, 0, 0, 0, 0))
    per_q = pl.BlockSpec((None, nh, t_new, dh), lambda i: (i, 0, 0, 0))
    o, l, nc = pl.pallas_call(
        functools.partial(_cache_attn_kernel, window // dilation, dilation),
        grid=(b,),
        in_specs=[win, per_q, pl.BlockSpec((None, 2, nh, dh, t_new), lambda i: (i, 0, 0, 0, 0))],
        out_specs=[per_q, per_q, win],
        out_shape=[SDS((b, nh, t_new, dh), F32), SDS((b, nh, t_new, dh), F32), SDS((b, 2, nh, dh, lw), F32)],
        compiler_params=_cp("parallel"),
        name=f"cache_attn_d{dilation}",
    )(ct, qh, kvt)
    tok = lambda a: jnp.transpose(a, (0, 2, 1, 3)).reshape(b * t_new, nh * dh)
    return tok(o), tok(l), jnp.transpose(nc, (0, 4, 1, 2, 3))


def _log_sigmoid(x):
    return jnp.minimum(x, 0.0) - jnp.log1p(jnp.exp(-jnp.abs(x)))


def _mlstm_kernel(n_valid, nsub, *refs):
    for u in range(nsub):
        _mlstm_chunk(n_valid, nsub, u, *refs)


def _mlstm_chunk(n_valid, nsub, u, q_ref, k_ref, v_ref, o_ref, g_ref, bif_ref, ng_ref, c0_ref, n0_ref, m0_ref,
                 h_ref, c1_ref, n1_ref, m1_ref, c_sc, n_sc, m_sc):
    ci = pl.program_id(1)
    nh, dk, dv = MLSTM_HEADS, MLSTM_DK, MLSTM_DV
    L = MLSTM_CHUNK

    @pl.when((ci == 0) & (u == 0))
    def _():
        c_sc[...] = jnp.zeros(c_sc.shape, F32)
        n_sc[...] = jnp.zeros(n_sc.shape, F32)
        m_sc[...] = jnp.zeros(m_sc.shape, F32)
        c_sc[:, 0:dk, :] = c0_ref[...]
        n_sc[:, :, 0:dk] = n0_ref[...]
        m_sc[...] = m0_ref[...]

    def rows(ref):
        x = ref[u * n_valid:(u + 1) * n_valid, :].astype(F32)
        if n_valid < L:
            x = jnp.concatenate([x, jnp.zeros((L - n_valid, x.shape[1]), F32)], axis=0)
        return x

    q = rows(q_ref).astype(BF16)
    k = rows(k_ref)
    v = rows(v_ref).astype(BF16)
    og = rows(o_ref)
    graw = rows(g_ref) + bif_ref[...]
    lane = lax.broadcasted_iota(I32, (L, LANES), 1)
    row = lax.broadcasted_iota(I32, (L, LANES), 0)
    is_i = lane < nh
    is_f = (lane >= nh) & (lane < 2 * nh)
    live = row < n_valid
    gi = jnp.where(is_i, jnp.where(live, graw, NEG_INF), 0.0)
    gf = jnp.where(is_f & live, _log_sigmoid(graw), 0.0)
    tril = (lax.broadcasted_iota(I32, (L, L), 0) >= lax.broadcasted_iota(I32, (L, L), 1))
    bcol = jnp.dot(tril.astype(F32), gf, precision=HIGHEST, preferred_element_type=F32)
    brow = bcol.T
    irow = gi.T
    scale = dk ** -0.5

    def heads(x):
        return jnp.stack([x[:, h * LANES:(h + 1) * LANES] for h in range(nh)], axis=0)

    def bdot(a, b, ca, cb):
        return lax.dot_general(a, b, (((ca,), (cb,)), ((0,), (0,))), preferred_element_type=F32)

    q3, v3 = heads(q), heads(v)
    k3 = heads(k)
    k3b = k3.astype(BF16)
    b_c = jnp.stack([bcol[:, nh + h:nh + h + 1] for h in range(nh)], axis=0)
    b_r = jnp.stack([brow[nh + h:nh + h + 1, :] for h in range(nh)], axis=0)
    i_r = jnp.stack([irow[h:h + 1, :] for h in range(nh)], axis=0)
    m_prev = m_sc[...]
    c3 = c_sc[...]
    n3 = n_sc[...]
    dlog = jnp.where(tril[None], b_c - b_r + i_r, NEG_INF)
    m_inter = b_c + m_prev
    m_t = jnp.maximum(m_inter, jnp.max(dlog, axis=-1, keepdims=True))
    sc = bdot(q3, k3b, 2, 2) * scale * jnp.exp(dlog - m_t)
    inter = jnp.exp(m_inter - m_t)
    num = bdot(sc.astype(BF16), v3, 2, 1) + inter * (bdot(q3, c3.astype(BF16), 2, 1) * scale)
    qn = jnp.sum(q3.astype(F32) * n3, axis=-1, keepdims=True) * scale
    den = jnp.sum(sc, axis=-1, keepdims=True) + inter * qn
    hh = num / jnp.maximum(jnp.abs(den), jnp.exp(-m_t))
    b_last = b_c[:, L - 1:L, :]
    g_r = b_last - b_r + i_r
    m_new = jnp.maximum(b_last + m_prev, jnp.max(g_r, axis=-1, keepdims=True))
    ws = jnp.exp(g_r - m_new)
    decay = jnp.exp(b_last + m_prev - m_new)
    kw = (jnp.stack([k3[h].T for h in range(nh)], axis=0) * ws).astype(BF16)
    c_sc[...] = decay * c3 + bdot(kw, v3, 2, 1)
    ws8 = jnp.broadcast_to(ws, (nh, 8, L)).astype(BF16)
    n_sc[...] = decay * n3 + bdot(ws8, k3b, 2, 1)[:, 0:1, :]
    m_sc[...] = m_new
    mu = jnp.mean(hh, axis=-1, keepdims=True)
    var = jnp.mean(jnp.square(hh - mu), axis=-1, keepdims=True)
    out = jax.nn.sigmoid(heads(og)) * ((hh - mu) * lax.rsqrt(var + EPS) * heads(ng_ref[...]))
    for h in range(nh):
        h_ref[u * n_valid:(u + 1) * n_valid, h * LANES:(h + 1) * LANES] = out[h, 0:n_valid, :].astype(h_ref.dtype)

    @pl.when((ci == pl.num_programs(1) - 1) & (u == nsub - 1))
    def _():
        c1_ref[...] = c_sc[:, 0:dk, :]
        n1_ref[...] = n_sc[:, :, 0:dk]
        m1_ref[...] = m_sc[...]


def _mlstm(qm, km, vm, om, gates, bif_pad, norm_g, c0, n0, m0, n_valid, out_dtype, name):
    b, s, w = qm.shape
    nh, dk, dv = MLSTM_HEADS, MLSTM_DK, MLSTM_DV
    nc = s // n_valid
    assert s % n_valid == 0 and (n_valid == MLSTM_CHUNK or nc == 1)
    nsub = MLSTM_STEP_CHUNKS if nc % MLSTM_STEP_CHUNKS == 0 else 1
    nc //= nsub
    tok = lambda width: pl.BlockSpec((None, nsub * n_valid, width), lambda i, c: (i, c, 0))
    const = lambda shape: pl.BlockSpec(shape, lambda i, c: (0,) * len(shape))
    h, c1, n1, m1 = pl.pallas_call(
        functools.partial(_mlstm_kernel, n_valid, nsub),
        grid=(b, nc),
        in_specs=[tok(w), tok(w), tok(w), tok(w), tok(LANES), const((1, LANES)), const((1, w)),
                  pl.BlockSpec((None, nh, dk, dv), lambda i, c: (i, 0, 0, 0)),
                  pl.BlockSpec((None, nh, 1, dk), lambda i, c: (i, 0, 0, 0)),
                  pl.BlockSpec((None, nh, 1, 1), lambda i, c: (i, 0, 0, 0))],
        out_specs=[tok(w),
                   pl.BlockSpec((None, nh, dk, dv), lambda i, c: (i, 0, 0, 0)),
                   pl.BlockSpec((None, nh, 1, dk), lambda i, c: (i, 0, 0, 0)),
                   pl.BlockSpec((None, nh, 1, 1), lambda i, c: (i, 0, 0, 0))],
        out_shape=[SDS((b, s, w), out_dtype), SDS((b, nh, dk, dv), F32), SDS((b, nh, 1, dk), F32),
                   SDS((b, nh, 1, 1), F32)],
        scratch_shapes=[pltpu.VMEM((nh, DK_PAD, dv), F32), pltpu.VMEM((nh, 1, DK_PAD), F32),
                        pltpu.VMEM((nh, 1, 1), F32)],
        compiler_params=_cp("parallel", "arbitrary"),
        name=name,
    )(qm, km, vm, om, gates, bif_pad, norm_g, c0, n0.reshape(b, nh, 1, dk), m0.reshape(b, nh, 1, 1))
    return h, c1, n1.reshape(b, nh, dk), m1.reshape(b, nh)


def _pack_bf16_pairs(x):
    w = x.shape[1] // 2
    bits = lax.bitcast_convert_type(x.astype(BF16).astype(F32), U32)
    return (bits[:, :w] >> 16) | (bits[:, w:] & jnp.uint32(0xFFFF0000))


def _unpack_pairs_f32(p):
    return lax.bitcast_convert_type(p << 16, F32), lax.bitcast_convert_type(p & jnp.uint32(0xFFFF0000), F32)


def _unpack_bf16_pairs(p):
    lo, hi = _unpack_pairs_f32(p)
    return lo.astype(BF16), hi.astype(BF16)


def _merge_kernel(x_ref, sc1_ref, sh1_ref, g1_ref, sc2_ref, sh2_ref,
                  o0_ref, o1_ref, o2_ref, l0_ref, l1_ref, l2_ref, mo_ref,
                  wg_ref, wpa_ref, wpm_ref, wout_ref, lng_ref, lnb_ref, x1_ref, h2p_ref, tok_sc):
    x = x_ref[...]
    d = x.shape[1]
    h = (x * (1.0 + sc1_ref[0]) + sh1_ref[0]).astype(BF16)
    g = jnp.dot(h, wg_ref[...], preferred_element_type=F32)

    def tokens(ref):
        dil, rows, _ = ref.shape
        if dil == 1:
            return ref[0]
        c = tok_sc.shape[0]
        for r in range(dil):
            for j in range(c):
                tok_sc[j, pl.ds(r, rows, stride=dil), :] = ref[r, :, j * LANES:(j + 1) * LANES]
        return jnp.concatenate([tok_sc[j] for j in range(c)], axis=1)

    l0, l1, l2 = tokens(l0_ref), tokens(l1_ref), tokens(l2_ref)
    lm = jnp.maximum(jnp.maximum(l0, l1), l2)
    e0, e1, e2 = jnp.exp(l0 - lm), jnp.exp(l1 - lm), jnp.exp(l2 - lm)
    ao = (e0 * tokens(o0_ref) + e1 * tokens(o1_ref) + e2 * tokens(o2_ref)) / (e0 + e1 + e2)
    pa = jnp.dot(ao.astype(BF16), wpa_ref[...], preferred_element_type=F32)
    pm = jnp.dot(mo_ref[...].astype(BF16), wpm_ref[...], preferred_element_type=F32)
    merged = jax.nn.sigmoid(g[:, :d]) * pa + jax.nn.sigmoid(g[:, d:]) * pm
    y = jnp.dot(merged.astype(BF16), wout_ref[...], preferred_element_type=F32)
    x1 = _layer_norm(ALPHA * x + g1_ref[0] * y, lng_ref[...], lnb_ref[...])
    x1_ref[...] = x1
    h2p_ref[...] = _pack_bf16_pairs(x1 * (1.0 + sc2_ref[0]) + sh2_ref[0])


def _merge(x, mods, attn, mo, wg, wpa, wpm, wout, lng, lnb, tm, tiles_per_batch, name):
    n, d = x.shape
    tok = lambda width: pl.BlockSpec((tm, width), lambda i: (i, 0))
    const = lambda a: pl.BlockSpec(a.shape, lambda i: (0,) * a.ndim)

    def lattice(a):
        dil, lat = a.shape[1], a.shape[2]
        per_batch = dil * lat // tm
        return pl.BlockSpec((None, dil, tm // dil, a.shape[3]), lambda i: (i // per_batch, 0, i % per_batch, 0))

    return pl.pallas_call(
        _merge_kernel,
        grid=(n // tm,),
        in_specs=[tok(d)] + [_mod_spec(m, tm, tiles_per_batch) for m in mods]
        + [lattice(a) for a in attn] + [tok(mo.shape[1])]
        + [const(a) for a in (wg, wpa, wpm, wout, lng, lnb)],
        out_specs=[tok(d), tok(d // 2)],
        out_shape=[SDS((n, d), F32), SDS((n, d // 2), U32)],
        scratch_shapes=[pltpu.VMEM((GROUP_WIDTH // LANES, tm, LANES), F32)],
        compiler_params=_cp("parallel"),
        name=name,
    )(x, *mods, *attn, mo, wg, wpa, wpm, wout, lng, lnb)


def _first_index_of_max(x, idx, big):
    mx = jnp.max(x, axis=0, keepdims=True)
    first = jnp.min(jnp.where(x == mx, idx, big), axis=0, keepdims=True)
    return mx, first


def _route_kernel(hp_ref, wlo_ref, whi_ref, bias_ref, idx_ref, w_ref, rank_ref, cnt_ref):
    tm = hp_ref.shape[0]
    ne = N_EXPERTS
    per = ne // N_EXPERT_GROUPS
    lo, hi = _unpack_bf16_pairs(hp_ref[...])
    logits = _nt_dot(wlo_ref[...], lo) + _nt_dot(whi_ref[...], hi)
    scores = jax.nn.sigmoid(logits)
    sel = scores + bias_ref[...]
    erow = lax.broadcasted_iota(I32, (ne, tm), 0).astype(F32)
    prow = lax.broadcasted_iota(I32, (per, tm), 0).astype(F32)
    gs = []
    for g in range(N_EXPERT_GROUPS):
        xg = sel[g * per:(g + 1) * per, :]
        m1, i1 = _first_index_of_max(xg, prow, per)
        m2 = jnp.max(jnp.where(prow == i1, NEG_INF, xg), axis=0, keepdims=True)
        gs.append(m1 + m2)
    gs = jnp.concatenate(gs, axis=0)
    grow = lax.broadcasted_iota(I32, gs.shape, 0).astype(F32)
    gkeep = jnp.zeros(gs.shape, F32)
    for _ in range(TOPK_GROUPS):
        _, gi = _first_index_of_max(gs, grow, N_EXPERT_GROUPS)
        pick = grow == gi
        gkeep = jnp.where(pick, 1.0, gkeep)
        gs = jnp.where(pick, NEG_INF, gs)
    keep = jnp.concatenate([jnp.broadcast_to(gkeep[g:g + 1, :], (per, tm)) for g in range(N_EXPERT_GROUPS)], axis=0)
    cand = jnp.where(keep > 0, sel, NEG_INF)
    member = jnp.zeros((ne, tm), F32)
    picks, idxs, ws = [], [], []
    for _ in range(TOP_K):
        _, ei = _first_index_of_max(cand, erow, ne)
        pick = erow == ei
        picks.append(pick)
        idxs.append(ei)
        ws.append(jnp.sum(jnp.where(pick, scores, 0.0), axis=0, keepdims=True))
        cand = jnp.where(pick, NEG_INF, cand)
        member = jnp.where(pick, 1.0, member)
    wsum = ws[0]
    for wk in ws[1:]:
        wsum = wsum + wk
    idx_ref[...] = jnp.concatenate(idxs, axis=0).astype(I32)
    w_ref[...] = jnp.concatenate(ws, axis=0) / wsum * ROUTED_SCALE
    ti = lax.broadcasted_iota(I32, (tm, tm), 0)
    tj = lax.broadcasted_iota(I32, (tm, tm), 1)
    before = (ti < tj).astype(BF16)
    mb = member.astype(BF16)
    prefix = jnp.dot(mb, before, preferred_element_type=F32)
    rank_ref[...] = jnp.concatenate(
        [jnp.sum(jnp.where(p, prefix, 0.0), axis=0, keepdims=True) for p in picks], axis=0).astype(I32)
    cnt_ref[...] = _nt_dot(jnp.ones((8, tm), BF16), mb).astype(I32)


def _route(h2p, w_router_t, bias_col):
    n, half = h2p.shape
    tm = ROUTE_TILE
    nt = n // tm
    wlo = w_router_t[:, :half]
    whi = w_router_t[:, half:]
    pair = lambda dt: SDS((TOP_K, n), dt)
    return pl.pallas_call(
        _route_kernel,
        grid=(nt,),
        in_specs=[pl.BlockSpec((tm, half), lambda i: (i, 0)),
                  pl.BlockSpec(wlo.shape, lambda i: (0, 0)),
                  pl.BlockSpec(whi.shape, lambda i: (0, 0)),
                  pl.BlockSpec(bias_col.shape, lambda i: (0, 0))],
        out_specs=[pl.BlockSpec((TOP_K, tm), lambda i: (0, i))] * 3
        + [pl.BlockSpec((None, 8, N_EXPERTS), lambda i: (i, 0, 0))],
        out_shape=[pair(I32), pair(F32), pair(I32), SDS((nt, 8, N_EXPERTS), I32)],
        compiler_params=_cp("parallel"),
        name="route",
    )(h2p, wlo, whi, bias_col)


def _dest_kernel(idx_ref, rank_ref, base_ref, dest_ref):
    tm = idx_ref.shape[1]
    erow = lax.broadcasted_iota(I32, (N_EXPERTS, tm), 0)
    base = base_ref[...]
    idx = idx_ref[...]
    rows = [jnp.sum(jnp.where(erow == idx[k:k + 1, :], base, 0.0), axis=0, keepdims=True) for k in range(TOP_K)]
    dest = jnp.concatenate(rows, axis=0).astype(I32) + rank_ref[...]
    tile = dest_ref.shape[2]
    for j in range(dest_ref.shape[0]):
        dest_ref[j] = dest[:, j * tile:(j + 1) * tile]


def _dest(idx, rank, base_cols, tile):
    n = idx.shape[1]
    per_route = ROUTE_TILE // tile
    return pl.pallas_call(
        _dest_kernel,
        grid=(n // ROUTE_TILE,),
        in_specs=[pl.BlockSpec((TOP_K, ROUTE_TILE), lambda i: (0, i)),
                  pl.BlockSpec((TOP_K, ROUTE_TILE), lambda i: (0, i)),
                  pl.BlockSpec((None, N_EXPERTS, 1), lambda i: (i, 0, 0))],
        out_specs=pl.BlockSpec((per_route, TOP_K, tile), lambda i: (i, 0, 0)),
        out_shape=SDS((n // tile, TOP_K, tile), I32),
        compiler_params=_cp("parallel"),
        name="dest",
    )(idx, rank, base_cols)


def _row_copy_loop(tile, make_copy, start):
    def body(t, carry):
        for k in range(TOP_K):
            cp = make_copy(k, t)
            if start:
                cp.start(priority=k % 2)
            else:
                cp.wait()
        return carry
    lax.fori_loop(0, tile, body, 0, unroll=8)


def _dest_fetch(dest_hbm, tile_idx, dest_smem, sem_idx):
    n_idx = dest_smem.shape[0]
    start = tile_idx * n_idx
    if not isinstance(start, int):
        start = pl.multiple_of(start, n_idx)
    return pltpu.make_async_copy(dest_hbm.at[pl.ds(start, n_idx)], dest_smem, sem_idx)


def _rows_to_pieces(x, ref):
    c = x.shape[1] // LANES
    for j in range(c):
        ref[pl.ds(j, x.shape[0], stride=c), :] = x[:, j * LANES:(j + 1) * LANES]


def _pieces_to_rows(ref, rows, c):
    return jnp.concatenate([ref[pl.ds(j, rows, stride=c), :] for j in range(c)], axis=1)


def _piece(ref, row, c):
    return ref.at[pl.ds(pl.multiple_of(row * c, c), c)]


def _scatter_kernel(pends_ref, dest_hbm, hp_ref, xs_out, stage, zeros_sc, dest_smem, sem_idx, sem_rows, sem_zero):
    i = pl.program_id(0)
    n = pl.num_programs(0)
    tile = hp_ref.shape[0]
    c = hp_ref.shape[1] // LANES
    slot = lax.rem(i, 2)
    idx_cp = _dest_fetch(dest_hbm, i, dest_smem, sem_idx)
    idx_cp.start()

    @pl.when(i == 0)
    def _():
        blk = zeros_sc.shape[0]
        zeros_sc[...] = jnp.zeros(zeros_sc.shape, zeros_sc.dtype)

        def zero_copy(e):
            start = jnp.maximum(pends_ref[e] * c - blk, 0)
            return pltpu.make_async_copy(zeros_sc, xs_out.at[pl.ds(pl.multiple_of(start, blk), blk)], sem_zero)

        def start_body(e, carry):
            zero_copy(e).start()
            return carry

        def wait_body(e, carry):
            zero_copy(e).wait()
            return carry

        lax.fori_loop(0, N_EXPERTS, start_body, 0)
        lax.fori_loop(0, N_EXPERTS, wait_body, 0)

    def row_copy(k, t):
        return pltpu.make_async_copy(_piece(stage.at[slot], t, c), _piece(xs_out, dest_smem[k * tile + t], c),
                                     sem_rows.at[slot])

    def wait_tile(s):
        _row_copy_loop(tile, lambda k, t: pltpu.make_async_copy(
            stage.at[s, pl.ds(0, c)], xs_out.at[pl.ds(0, c)], sem_rows.at[s]), False)

    @pl.when(i >= 2)
    def _():
        wait_tile(slot)

    _rows_to_pieces(hp_ref[...], stage.at[slot])
    idx_cp.wait()
    _row_copy_loop(tile, row_copy, True)

    @pl.when(i == n - 1)
    def _():
        @pl.when(n >= 2)
        def _():
            wait_tile(1 - slot)
        wait_tile(slot)


def _scatter_rows(pends, dest_flat, h2p, n_rows):
    n, half = h2p.shape
    tile = MOE_TILE
    c = half // LANES
    grid_spec = pltpu.PrefetchScalarGridSpec(
        num_scalar_prefetch=1,
        grid=(n // tile,),
        in_specs=[pl.BlockSpec(memory_space=pl.ANY),
                  pl.BlockSpec((tile, half), lambda i, pe: (i, 0))],
        out_specs=pl.BlockSpec(memory_space=pl.ANY),
        scratch_shapes=[pltpu.VMEM((2, tile * c, LANES), U32), pltpu.VMEM((MOE_BLOCK * c, LANES), U32),
                        pltpu.SMEM((TOP_K * tile,), I32),
                        pltpu.SemaphoreType.DMA, pltpu.SemaphoreType.DMA((2,)), pltpu.SemaphoreType.DMA],
    )
    return pl.pallas_call(
        _scatter_kernel,
        grid_spec=grid_spec,
        out_shape=SDS((n_rows * c, LANES), U32),
        compiler_params=_cp("arbitrary"),
        name="scatter_rows",
    )(pends, dest_flat, h2p)


def _expert_kernel(blk_exp_ref, n_used_ref, xs_ref, win_ref, wout_ref, ys_ref, win_sc, wout_sc):
    i = pl.program_id(0)
    prev = blk_exp_ref[jnp.maximum(i - 1, 0)]

    @pl.when((i == 0) | (blk_exp_ref[i] != prev))
    def _():
        win_sc[...] = win_ref[...].astype(BF16)
        wout_sc[...] = wout_ref[...].astype(BF16)

    @pl.when(i < n_used_ref[0])
    def _():
        half = win_sc.shape[0] // 2
        lo, hi = _unpack_bf16_pairs(_pieces_to_rows(xs_ref, MOE_BLOCK, half // LANES))
        ag = (jnp.dot(lo, win_sc[0:half, :], preferred_element_type=F32)
              + jnp.dot(hi, win_sc[half:2 * half, :], preferred_element_type=F32))
        a = ag[:, :D_EXPERT]
        g = ag[:, D_EXPERT:]
        mid = (a * jax.nn.sigmoid(a) * g).astype(BF16)
        _rows_to_pieces(_pack_bf16_pairs(jnp.dot(mid, wout_sc[...], preferred_element_type=F32)), ys_ref)

    @pl.when(i >= n_used_ref[0])
    def _():
        ys_ref[...] = jnp.zeros(ys_ref.shape, ys_ref.dtype)


def _experts(xs, blk_exp, n_used, w_exp_in, w_exp_out):
    d = w_exp_in.shape[1]
    cx = cy = d // 2 // LANES
    n_rows = xs.shape[0] // cx
    nb = n_rows // MOE_BLOCK
    grid_spec = pltpu.PrefetchScalarGridSpec(
        num_scalar_prefetch=2,
        grid=(nb,),
        in_specs=[pl.BlockSpec((MOE_BLOCK * cx, LANES), lambda i, be, nu: (jnp.where(i < nu[0], i, 0), 0)),
                  pl.BlockSpec((None, d, 2 * D_EXPERT), lambda i, be, nu: (be[i], 0, 0)),
                  pl.BlockSpec((None, D_EXPERT, d), lambda i, be, nu: (be[i], 0, 0))],
        out_specs=pl.BlockSpec((MOE_BLOCK * cy, LANES), lambda i, be, nu: (i, 0)),
        scratch_shapes=[pltpu.VMEM((d, 2 * D_EXPERT), BF16), pltpu.VMEM((D_EXPERT, d), BF16)],
    )
    return pl.pallas_call(
        _expert_kernel,
        grid_spec=grid_spec,
        out_shape=SDS((n_rows * cy, LANES), U32),
        compiler_params=_cp("arbitrary"),
        name="experts",
    )(blk_exp, n_used, xs, w_exp_in, w_exp_out)


def _final_kernel(dest_hbm, x1_ref, hp_ref, w_ref, g2_ref, ys_hbm, wsin_ref, wsout_ref, lng_ref, lnb_ref,
                  y_ref, rows_sc, dest_smem, sem_idx, sem_rows, *, tile_offset):
    i = pl.program_id(0)
    n = pl.num_programs(0)
    tile = x1_ref.shape[0]
    c = hp_ref.shape[1] // LANES
    slot = lax.rem(i, 2)

    def start_rows(s):
        _row_copy_loop(tile, lambda k, t: pltpu.make_async_copy(
            _piece(ys_hbm, dest_smem[k * tile + t], c), _piece(rows_sc.at[s, k], t, c), sem_rows.at[s]), True)

    @pl.when(i == 0)
    def _():
        first = _dest_fetch(dest_hbm, tile_offset, dest_smem, sem_idx)
        first.start()
        first.wait()
        start_rows(0)

    nxt = _dest_fetch(dest_hbm, jnp.minimum(i + 1, n - 1) + tile_offset, dest_smem, sem_idx)

    @pl.when(i + 1 < n)
    def _():
        nxt.start()

    half = hp_ref.shape[1]
    lo, hi = _unpack_bf16_pairs(hp_ref[...])
    ag = (jnp.dot(lo, wsin_ref[0:half, :], preferred_element_type=F32)
          + jnp.dot(hi, wsin_ref[half:2 * half, :], preferred_element_type=F32))
    a = ag[:, :D_SHARED]
    g = ag[:, D_SHARED:]
    y2 = jnp.dot((a * jax.nn.sigmoid(a) * g).astype(BF16), wsout_ref[...], preferred_element_type=F32)
    w_cols = jnp.concatenate([w_ref[...], jnp.zeros((tile - TOP_K, tile), F32)], axis=0).T

    @pl.when(i + 1 < n)
    def _():
        nxt.wait()
        start_rows(1 - slot)

    _row_copy_loop(tile, lambda k, t: pltpu.make_async_copy(
        ys_hbm.at[pl.ds(0, c)], rows_sc.at[slot, 0, pl.ds(0, c)], sem_rows.at[slot]), False)
    y_lo, y_hi = y2[:, :half], y2[:, half:]
    for k in range(TOP_K):
        r_lo, r_hi = _unpack_pairs_f32(_pieces_to_rows(rows_sc.at[slot, k], tile, c))
        y_lo = y_lo + w_cols[:, k:k + 1] * r_lo
        y_hi = y_hi + w_cols[:, k:k + 1] * r_hi
    y2 = jnp.concatenate([y_lo, y_hi], axis=1)
    y_ref[...] = _layer_norm(ALPHA * x1_ref[...] + g2_ref[0] * y2, lng_ref[...], lnb_ref[...])


def _final(dest_flat, x1, h2p, w_top, g2_3, ys, wsin, wsout, lng, lnb, tile_offset, tiles_per_batch, name):
    n, d = x1.shape
    tile = MOE_TILE
    assert tile == LANES
    const = lambda a: pl.BlockSpec(a.shape, lambda i: (0,) * a.ndim)
    return pl.pallas_call(
        functools.partial(_final_kernel, tile_offset=tile_offset),
        grid=(n // tile,),
        in_specs=[pl.BlockSpec(memory_space=pl.ANY),
                  pl.BlockSpec((tile, d), lambda i: (i, 0)),
                  pl.BlockSpec((tile, d // 2), lambda i: (i, 0)),
                  pl.BlockSpec((TOP_K, tile), lambda i: (0, i)),
                  _mod_spec(g2_3, tile, tiles_per_batch),
                  pl.BlockSpec(memory_space=pl.ANY),
                  const(wsin), const(wsout), const(lng), const(lnb)],
        out_specs=pl.BlockSpec((tile, d), lambda i: (i, 0)),
        out_shape=SDS((n, d), F32),
        scratch_shapes=[pltpu.VMEM((2, TOP_K, tile * (d // 2 // LANES), LANES), U32), pltpu.SMEM((TOP_K * tile,), I32),
                        pltpu.SemaphoreType.DMA, pltpu.SemaphoreType.DMA((2,))],
        compiler_params=_cp("arbitrary"),
        name=name,
    )(dest_flat, x1, h2p, w_top, g2_3, ys, wsin, wsout, lng, lnb)


def _sorted_layout(counts):
    totals = jnp.sum(counts, axis=0)
    padded = (totals + MOE_BLOCK - 1) // MOE_BLOCK * MOE_BLOCK
    pends = jnp.cumsum(padded)
    base = (pends - padded)[None, :] + jnp.cumsum(counts, axis=0) - counts
    return base.astype(I32), pends


def _moe_routed(h2p, w_router, router_bias, w_exp_in, w_exp_out):
    n = h2p.shape[0]
    idx, w_top, rank, cnt = _route(h2p, w_router.T.astype(BF16), router_bias.reshape(N_EXPERTS, 1))
    base, pends = _sorted_layout(cnt[:, 0, :])
    n_blocks = -(-(n * TOP_K + N_EXPERTS * (MOE_BLOCK - 1)) // MOE_BLOCK)
    blk_start = jnp.arange(n_blocks, dtype=I32) * MOE_BLOCK
    blk_exp = jnp.minimum(jnp.sum((pends[None, :] <= blk_start[:, None]).astype(I32), axis=1), N_EXPERTS - 1)
    n_used = (pends[-1:] // MOE_BLOCK).astype(I32)
    dest = _dest(idx, rank, base.astype(F32)[:, :, None], MOE_TILE).reshape(-1)
    xs = _scatter_rows(pends.astype(I32), dest, h2p, n_blocks * MOE_BLOCK)
    ys = _experts(xs, blk_exp, n_used, w_exp_in, w_exp_out)
    return dest, w_top, ys


def _pad_heads(w, nh, dk):
    d = w.shape[0]
    return jnp.pad(w.reshape(d, nh, dk), ((0, 0), (0, 0), (0, DK_PAD - dk))).reshape(d, nh * DK_PAD)


def _split_w_in(w_in):
    aw = N_GROUPS * GROUP_WIDTH
    nh, dk, dv = MLSTM_HEADS, MLSTM_DK, MLSTM_DV
    offs = np.cumsum([0, aw, aw, aw, nh * dk, nh * dk, nh * dv, nh * dv, nh, nh, w_in.shape[0], w_in.shape[0]])
    seg = [w_in[:, offs[i]:offs[i + 1]] for i in range(11)]
    q_a, k_a, v_a, q_m, k_m, v_m, o_m, i_m, f_m, g_a, g_b = seg
    bf = lambda a: a.astype(BF16)
    w_gates = jnp.pad(jnp.concatenate([i_m, f_m], axis=1), ((0, 0), (0, LANES - 2 * nh)))
    w_gates_hi = w_gates.astype(BF16)
    w_gates = jnp.concatenate([w_gates_hi, (w_gates - w_gates_hi.astype(F32)).astype(BF16)], axis=1)
    return dict(qa=bf(q_a), kva=bf(jnp.concatenate([k_a, v_a], axis=1)),
                qm=bf(_pad_heads(q_m, nh, dk)), km=bf(_pad_heads(k_m, nh, dk)), vm=bf(v_m), om=bf(o_m),
                gates=w_gates, g=bf(jnp.concatenate([g_a, g_b], axis=1)))


def _mixing(x, mods, tm, tiles_per_batch, batch, seq, caches, states, wts, prompt):
    n, d = x.shape
    sc1, sh1, g1 = mods["scale1"], mods["shift1"], mods["gate1"]
    sc2, sh2 = mods["scale2"], mods["shift2"]
    act = BF16 if prompt else F32
    if prompt:
        tail = min(max(w for w, _ in ATTN_GROUPS), seq)
        qkv_l, kv_tail = _proj_lattice(x, sc1, sh1, wts["qa"], wts["kva"], batch, seq, tm, tail)
    else:
        q_a, kv_a = _proj(x, sc1, sh1, [wts["qa"], wts["kva"]], [F32, F32], None, tm, tiles_per_batch, "proj_attn_s")
    qm, km, vm, om, gates = _proj(x, sc1, sh1, [wts["qm"], wts["km"], wts["vm"], wts["om"]],
                                  [act, act, act, F32], wts["gates"], tm, tiles_per_batch,
                                  "proj_mlstm_p" if prompt else "proj_mlstm_s")
    attn_o, attn_l, bufs = [], [], []
    for g, (window, dilation) in enumerate(ATTN_GROUPS):
        if prompt:
            o, l = _band_attn(*qkv_l[g], window)
            keep = min(window, seq)
            kv3 = kv_tail.reshape(batch, tail, 2, N_GROUPS, GROUP_WIDTH)[:, tail - keep:, :, g, :]
            bufs.append(kv3.reshape(1, batch, keep, 2, ATTN_HEADS, ATTN_HEAD_DIM))
        else:
            o, l, nc = _cache_attn(q_a.reshape(batch, seq, -1), kv_a.reshape(batch, seq, -1), caches[g], g,
                                   window, dilation)
            o, l = o[None, None], l[None, None]
            bufs.append(nc[None])
        attn_o.append(o)
        attn_l.append(l)
    r3 = lambda a: a.reshape(batch, seq, a.shape[-1])
    n_valid = MLSTM_CHUNK if prompt else seq
    mo, c1, n1, m1 = _mlstm(r3(qm), r3(km), r3(vm), r3(om), r3(gates), wts["bif"], wts["norm_g"],
                            states[0], states[1], states[2], n_valid, act,
                            "mlstm_p" if prompt else "mlstm_s")
    x1, h2p = _merge(x, [sc1, sh1, g1, sc2, sh2], attn_o + attn_l, mo.reshape(n, -1),
                     wts["g"], wts["pa"], wts["pm"], wts["out"], wts["ln1_g"], wts["ln1_b"],
                     tm, tiles_per_batch, "merge_p" if prompt else "merge_s")
    return x1, h2p, bufs, (c1[None], n1[None], m1[None])


def _mod_pieces(mod, d, rows_per_batch, tm):
    names = ("shift1", "scale1", "gate1", "shift2", "scale2", "gate2")
    out = {}
    for p, name in enumerate(names):
        piece = mod[:, p * d:(p + 1) * d]
        if rows_per_batch % tm == 0:
            out[name] = piece[:, None, :]
        else:
            out[name] = jnp.repeat(piece, rows_per_batch, axis=0).reshape(-1, tm, d)
    return out


def kernel(x_prompt, x_sample, cache_kv_w128, cache_kv_w512, cache_kv_w2048, state_mlstm_C, state_mlstm_n, state_mlstm_m, c_prompt, c_sample, w_ada, b_ada, w_in, b_if, mlstm_norm_g, w_proj_attn, w_proj_mlstm, w_out, ln1_g, ln1_b, w_router, router_bias, w_exp_in, w_exp_out, w_sh_in, w_sh_out, ln2_g, ln2_b):
    assert w_ada.shape[0] == DEPTH
    bp, sp, d = x_prompt.shape
    bs, ss, _ = x_sample.shape
    nh = MLSTM_HEADS
    np_, ns = bp * sp, bs * ss

    mod = _ada(jnp.concatenate([c_prompt, c_sample], axis=0), w_ada[0], b_ada[0])
    wts = _split_w_in(w_in[0])
    wts.update(
        bif=jnp.pad(b_if[0], (0, LANES - 2 * nh)).reshape(1, LANES),
        norm_g=mlstm_norm_g[0].reshape(1, -1),
        pa=w_proj_attn[0].astype(BF16), pm=w_proj_mlstm[0].astype(BF16), out=w_out[0].astype(BF16),
        ln1_g=ln1_g[0].reshape(1, d), ln1_b=ln1_b[0].reshape(1, d))

    tm_p, tm_s = 512, 256
    mods_p = _mod_pieces(mod[:bp], d, sp, tm_p)
    mods_s = _mod_pieces(mod[bp:], d, ss, tm_s)
    zeros_p = (jnp.zeros((bp, nh, MLSTM_DK, MLSTM_DV), F32), jnp.zeros((bp, nh, MLSTM_DK), F32),
               jnp.zeros((bp, nh), F32))
    x1p, h2p_p, bufs_p, st_p = _mixing(x_prompt.reshape(np_, d), mods_p, tm_p, sp // tm_p, bp, sp, None,
                                       zeros_p, wts, True)
    caches = (cache_kv_w128[0], cache_kv_w512[0], cache_kv_w2048[0])
    states = (state_mlstm_C[0], state_mlstm_n[0], state_mlstm_m[0])
    x1s, h2p_s, bufs_s, st_s = _mixing(x_sample.reshape(ns, d), mods_s, tm_s, 1, bs, ss, caches, states, wts, False)

    h2p = jnp.concatenate([h2p_p, h2p_s], axis=0)
    dest, w_top, ys = _moe_routed(h2p, w_router[0], router_bias[0], w_exp_in[0], w_exp_out[0])
    wsin, wsout = w_sh_in[0].astype(BF16), w_sh_out[0].astype(BF16)
    lng, lnb = ln2_g[0].reshape(1, d), ln2_b[0].reshape(1, d)
    g2_p = _mod_pieces(mod[:bp], d, sp, MOE_TILE)["gate2"]
    g2_s = _mod_pieces(mod[bp:], d, ss, MOE_TILE)["gate2"]
    y_p = _final(dest, x1p, h2p_p, w_top[:, :np_], g2_p, ys, wsin, wsout, lng, lnb, 0, sp // MOE_TILE, "final_p")
    y_s = _final(dest, x1s, h2p_s, w_top[:, np_:], g2_s, ys, wsin, wsout, lng, lnb, np_ // MOE_TILE, 1, "final_s")

    return (y_p.reshape(bp, sp, d), y_s.reshape(bs, ss, d),
            bufs_p[0], bufs_p[1], bufs_p[2], st_p[0], st_p[1], st_p[2],
            bufs_s[0], bufs_s[1], bufs_s[2], st_s[0], st_s[1], st_s[2])
```

```python
import functools

import numpy as np
import jax
import jax.numpy as jnp
from jax import lax
from jax.experimental import pallas as pl
from jax.experimental.pallas import tpu as pltpu

F32 = jnp.float32
BF16 = jnp.bfloat16
I32 = jnp.int32
U32 = jnp.uint32
HIGHEST = lax.Precision.HIGHEST
SDS = jax.ShapeDtypeStruct
NEG_INF = float("-inf")

ATTN_GROUPS = ((128, 1), (512, 4), (2048, 16))
N_GROUPS = 3
ATTN_HEADS = 4
ATTN_HEAD_DIM = 64
GROUP_WIDTH = ATTN_HEADS * ATTN_HEAD_DIM
ATTN_BLOCK = 128
ATTN_STEP_BLOCKS = 4
MLSTM_HEADS = 8
MLSTM_DK = 64
MLSTM_DV = 128
N_EXPERTS = 256
TOP_K = 8
N_EXPERT_GROUPS = 8
TOPK_GROUPS = 4
D_EXPERT = 256
D_SHARED = 256
ROUTED_SCALE = 2.5
DEPTH = 1
ALPHA = (2 * DEPTH) ** 0.25
EPS = 1e-5

LANES = 128
MLSTM_CHUNK = 128
MLSTM_STEP_CHUNKS = 2
DK_PAD = LANES
MOE_TILE = 128
MOE_BLOCK = 512
ROUTE_TILE = 512
VMEM_LIMIT = 56 << 20


def _cp(*sem, vmem=VMEM_LIMIT):
    return pltpu.CompilerParams(dimension_semantics=sem, vmem_limit_bytes=vmem)


def _nt_dot(a, b):
    return lax.dot_general(a, b, (((1,), (1,)), ((), ())), preferred_element_type=F32)


def _layer_norm(r, g, b):
    mu = jnp.mean(r, axis=-1, keepdims=True)
    var = jnp.mean(jnp.square(r - mu), axis=-1, keepdims=True)
    return (r - mu) * lax.rsqrt(var + EPS) * g + b


def _mod_spec(mod3, tm, tiles_per_batch):
    d = mod3.shape[-1]
    if mod3.shape[1] == 1:
        return pl.BlockSpec((1, 1, d), lambda i: (i // tiles_per_batch, 0, 0))
    return pl.BlockSpec((1, tm, d), lambda i: (i, 0, 0))


def _ada_kernel(c_ref, w_ref, b_ref, o_ref):
    c = c_ref[...]
    s = c * jax.nn.sigmoid(c)
    o_ref[...] = jnp.dot(s, w_ref[...], precision=HIGHEST, preferred_element_type=F32) + b_ref[...]


def _ada(c_all, w_ada, b_ada):
    r, d = c_all.shape
    n = w_ada.shape[1]
    tn = 1024
    return pl.pallas_call(
        _ada_kernel,
        grid=(n // tn,),
        in_specs=[pl.BlockSpec((r, d), lambda j: (0, 0)),
                  pl.BlockSpec((d, tn), lambda j: (0, j)),
                  pl.BlockSpec((1, tn), lambda j: (0, j))],
        out_specs=pl.BlockSpec((r, tn), lambda j: (0, j)),
        out_shape=SDS((r, n), F32),
        compiler_params=_cp("parallel"),
        name="ada",
    )(c_all, w_ada, b_ada.reshape(1, n))


def _proj_kernel(n_w, has_hp, x_ref, sc_ref, sh_ref, *refs):
    w_refs = refs[:n_w]
    pos = n_w
    if has_hp:
        whp_ref = refs[pos]
        pos += 1
    o_refs = refs[pos:pos + n_w]
    h = x_ref[...] * (1.0 + sc_ref[0]) + sh_ref[0]
    hb = h.astype(BF16)
    for w_ref, o_ref in zip(w_refs, o_refs):
        o_ref[...] = jnp.dot(hb, w_ref[...], preferred_element_type=F32).astype(o_ref.dtype)
    if has_hp:
        ohp_ref = refs[pos + n_w]
        wn = ohp_ref.shape[1]
        h_lo = (h - hb.astype(F32)).astype(BF16)
        r = jnp.dot(hb, whp_ref[...], preferred_element_type=F32)
        ohp_ref[...] = (r[:, :wn] + r[:, wn:]) + jnp.dot(h_lo, whp_ref[:, 0:wn], preferred_element_type=F32)


def _proj(x, sc3, sh3, weights, out_dtypes, w_hp, tm, tiles_per_batch, name):
    n, d = x.shape
    n_w = len(weights)
    has_hp = w_hp is not None
    in_specs = [pl.BlockSpec((tm, d), lambda i: (i, 0)),
                _mod_spec(sc3, tm, tiles_per_batch), _mod_spec(sh3, tm, tiles_per_batch)]
    in_specs += [pl.BlockSpec(w.shape, lambda i: (0, 0)) for w in weights]
    out_specs = [pl.BlockSpec((tm, w.shape[1]), lambda i: (i, 0)) for w in weights]
    out_shape = [SDS((n, w.shape[1]), dt) for w, dt in zip(weights, out_dtypes)]
    args = [x, sc3, sh3, *weights]
    if has_hp:
        in_specs.append(pl.BlockSpec(w_hp.shape, lambda i: (0, 0)))
        out_specs.append(pl.BlockSpec((tm, w_hp.shape[1] // 2), lambda i: (i, 0)))
        out_shape.append(SDS((n, w_hp.shape[1] // 2), F32))
        args.append(w_hp)
    return pl.pallas_call(
        functools.partial(_proj_kernel, n_w, has_hp),
        grid=(n // tm,),
        in_specs=in_specs, out_specs=out_specs, out_shape=out_shape,
        compiler_params=_cp("parallel"),
        name=name,
    )(*args)


def _head_masks(width):
    lane_head = lax.broadcasted_iota(I32, (1, width), 1) // ATTN_HEAD_DIM
    return [(lane_head == h).astype(F32) for h in range(ATTN_HEADS)]


def _band_attn_kernel(wband, q_ref, kc_ref, kp_ref, vc_ref, vp_ref, o_ref, l_ref):
    j = pl.program_id(2)
    qb = kp_ref.shape[0]
    nsub = q_ref.shape[0] // qb
    k_all = jnp.concatenate([kp_ref[...], kc_ref[...]], axis=0).astype(BF16)
    v_all = jnp.concatenate([vp_ref[...], vc_ref[...]], axis=0).astype(BF16)
    qi = lax.broadcasted_iota(I32, (qb, 2 * qb), 0)
    kj = lax.broadcasted_iota(I32, (qb, 2 * qb), 1)
    dist = qi + qb - kj
    band = jnp.where(dist >= 0, 1, 0) * jnp.where(dist <= wband, 1, 0)
    hms = _head_masks(q_ref.shape[1])
    for u in range(nsub):
        rows = slice(u * qb, (u + 1) * qb)
        q = q_ref[rows, :].astype(F32)
        k = k_all[u * qb:(u + 2) * qb]
        v = v_all[u * qb:(u + 2) * qb]
        valid = band * jnp.where((j * nsub + u) * qb + kj - qb >= 0, 1, 0)
        bias = jnp.where(valid > 0, 0.0, NEG_INF)
        qs = jnp.concatenate([q * (hm * ATTN_HEAD_DIM ** -0.5) for hm in hms], axis=0).astype(BF16)
        s = _nt_dot(qs, k) + jnp.concatenate([bias] * ATTN_HEADS, axis=0)
        mx = jnp.max(s, axis=-1, keepdims=True)
        p = jnp.exp(s - mx)
        den = jnp.sum(p, axis=-1, keepdims=True)
        o4 = jnp.dot((p / den).astype(BF16), v, preferred_element_type=F32)
        l4 = mx + jnp.log(den)
        o_acc = jnp.zeros((qb, o_ref.shape[1]), F32)
        l_acc = jnp.zeros((qb, l_ref.shape[1]), F32)
        for h, hm in enumerate(hms):
            o_acc = o_acc + o4[h * qb:(h + 1) * qb, :] * hm
            l_acc = l_acc + l4[h * qb:(h + 1) * qb, :] * hm
        o_ref[rows, :] = o_acc
        l_ref[rows, :] = l_acc


def _band_attn(q_l, k_l, v_l, window):
    batch, dilation, lat, gw = q_l.shape
    nsub = max(n for n in range(1, ATTN_STEP_BLOCKS + 1) if lat % (n * ATTN_BLOCK) == 0)
    nb = lat // (nsub * ATTN_BLOCK)
    blk = (None, None, nsub * ATTN_BLOCK, gw)
    blk_prev = (None, None, ATTN_BLOCK, gw)
    cur = lambda b, r, j: (b, r, j, 0)
    prev = lambda b, r, j: (b, r, jnp.maximum(nsub * j - 1, 0), 0)
    return pl.pallas_call(
        functools.partial(_band_attn_kernel, window // dilation),
        grid=(batch, dilation, nb),
        in_specs=[pl.BlockSpec(blk, cur), pl.BlockSpec(blk, cur), pl.BlockSpec(blk_prev, prev),
                  pl.BlockSpec(blk, cur), pl.BlockSpec(blk_prev, prev)],
        out_specs=[pl.BlockSpec(blk, cur), pl.BlockSpec(blk, cur)],
        out_shape=[SDS(q_l.shape, F32)] * 2,
        compiler_params=_cp("parallel", "parallel", "arbitrary"),
        name=f"band_attn_d{dilation}",
    )(q_l, k_l, k_l, v_l, v_l)


def _rows_to_lane_tiles(x, ref):
    for j in range(x.shape[1] // LANES):
        ref[j] = x[:, j * LANES:(j + 1) * LANES]


def _proj_lattice_kernel(dils, t0, x_ref, sc_ref, sh_ref, wq_ref, wkv_ref, *refs):
    ng = len(dils)
    lat_refs = refs[:3 * ng]
    kv_tail_ref = refs[3 * ng]
    zs = refs[3 * ng + 1]
    gw = GROUP_WIDTH
    tm = x_ref.shape[0]
    hb = (x_ref[...] * (1.0 + sc_ref[0]) + sh_ref[0]).astype(BF16)
    zq = jnp.dot(hb, wq_ref[...], preferred_element_type=F32)
    zkv = jnp.dot(hb, wkv_ref[...], preferred_element_type=F32)

    @pl.when(pl.program_id(1) >= t0)
    def _():
        kv_tail_ref[...] = zkv

    for g, d in enumerate(dils):
        for part, z in enumerate((zq[:, g * gw:(g + 1) * gw], zkv[:, g * gw:(g + 1) * gw],
                                  zkv[:, (ng + g) * gw:(ng + g + 1) * gw])):
            out = lat_refs[3 * g + part]
            if d == 1:
                out[0] = z.astype(out.dtype)
            else:
                _rows_to_lane_tiles(z, zs)
                for r in range(d):
                    out[r] = jnp.concatenate([zs[j, pl.ds(r, tm // d, stride=d), :] for j in range(gw // LANES)],
                                             axis=1).astype(out.dtype)


def _proj_lattice(x, sc3, sh3, wq, wkv, batch, seq, tm, tail):
    d_model = x.shape[1]
    tiles = seq // tm
    t0 = (seq - tail) // tm
    assert seq % tm == 0 and (seq - tail) % tm == 0
    dils = tuple(d for _, d in ATTN_GROUPS)
    gw = GROUP_WIDTH
    lat_specs, lat_shapes = [], []
    for d in dils:
        for _ in range(3):
            lat_specs.append(pl.BlockSpec((None, d, tm // d, gw), lambda b, t: (b, 0, t, 0)))
            lat_shapes.append(SDS((batch, d, seq // d, gw), BF16))
    outs = pl.pallas_call(
        functools.partial(_proj_lattice_kernel, dils, t0),
        grid=(batch, tiles),
        in_specs=[pl.BlockSpec((tm, d_model), lambda b, t: (b * tiles + t, 0)),
                  pl.BlockSpec((1, 1, d_model), lambda b, t: (b, 0, 0)),
                  pl.BlockSpec((1, 1, d_model), lambda b, t: (b, 0, 0)),
                  pl.BlockSpec(wq.shape, lambda b, t: (0, 0)),
                  pl.BlockSpec(wkv.shape, lambda b, t: (0, 0))],
        out_specs=lat_specs + [pl.BlockSpec((None, tm, wkv.shape[1]), lambda b, t: (b, jnp.maximum(t - t0, 0), 0))],
        out_shape=lat_shapes + [SDS((batch, tail, wkv.shape[1]), F32)],
        scratch_shapes=[pltpu.VMEM((gw // LANES, tm, LANES), F32)],
        compiler_params=_cp("parallel", "arbitrary"),
        name="proj_attn_p",
    )(x, sc3, sh3, wq, wkv)
    return [outs[3 * g:3 * g + 3] for g in range(len(dils))], outs[-1]


def _cache_attn_kernel(wband, dilation, c_ref, q_ref, kv_ref, o_ref, l_ref, nc_ref):
    lw = c_ref.shape[-1]
    t_new = q_ref.shape[1]
    dh = q_ref.shape[2]

    def band_bias(n_keys, key_base):
        t = lax.broadcasted_iota(I32, (t_new, n_keys), 0)
        p = lax.broadcasted_iota(I32, (t_new, n_keys), 1) + key_base
        delta = lw + t - p
        ok = (jnp.where(delta >= 0, 1, 0) * jnp.where((delta & (dilation - 1)) == 0, 1, 0)
              * jnp.where(delta <= wband * dilation, 1, 0))
        return jnp.where(ok > 0, 0.0, NEG_INF)

    bias_c = band_bias(lw, 0)
    bias_n = band_bias(t_new, lw)
    place = (lax.broadcasted_iota(I32, (t_new, LANES), 1)
             == lax.broadcasted_iota(I32, (t_new, LANES), 0) + (LANES - t_new)).astype(F32)
    tail = lax.broadcasted_iota(I32, (1, LANES), 1) >= LANES - t_new
    for h in range(ATTN_HEADS):
        qh = (q_ref[h] * dh ** -0.5).astype(BF16)
        kt, vt = c_ref[0, h], c_ref[1, h]
        knt, vnt = kv_ref[0, h], kv_ref[1, h]
        sc = jnp.dot(qh, kt.astype(BF16), preferred_element_type=F32) + bias_c
        sn = jnp.dot(qh, knt.astype(BF16), preferred_element_type=F32) + bias_n
        mx = jnp.maximum(jnp.max(sc, axis=-1, keepdims=True), jnp.max(sn, axis=-1, keepdims=True))
        pc = jnp.exp(sc - mx)
        pn = jnp.exp(sn - mx)
        den = jnp.sum(pc, axis=-1, keepdims=True) + jnp.sum(pn, axis=-1, keepdims=True)
        o_ref[h] = (_nt_dot((pc / den).astype(BF16), vt.astype(BF16))
                    + _nt_dot((pn / den).astype(BF16), vnt.astype(BF16)))
        l_ref[h] = jnp.broadcast_to(mx + jnp.log(den), (t_new, dh))
        for kv, (old, new) in enumerate(((kt, knt), (vt, vnt))):
            rolled = pltpu.roll(old, lw - t_new, axis=1)
            new_tile = jnp.dot(new, place, precision=HIGHEST, preferred_element_type=F32)
            if lw > LANES:
                nc_ref[kv, h, :, 0:lw - LANES] = rolled[:, 0:lw - LANES]
            nc_ref[kv, h, :, lw - LANES:lw] = jnp.where(tail, new_tile, rolled[:, lw - LANES:lw])


def _cache_attn(q_s, kv_s, cache, g, window, dilation):
    b, lw = cache.shape[0], cache.shape[1]
    t_new = q_s.shape[1]
    nh, dh = ATTN_HEADS, ATTN_HEAD_DIM
    assert lw == window and lw % LANES == 0 and t_new % 8 == 0 and dilation & (dilation - 1) == 0
    ct = jnp.transpose(cache, (0, 2, 3, 4, 1))
    qh = jnp.transpose(q_s.reshape(b, t_new, N_GROUPS, nh, dh)[:, :, g], (0, 2, 1, 3))
    kvt = jnp.transpose(kv_s.reshape(b, t_new, 2, N_GROUPS, nh, dh)[:, :, :, g], (0, 2, 3, 4, 1))
    win = pl.BlockSpec((None, 2, nh, dh, lw), lambda i: (i, 0, 0, 0, 0))
    per_q = pl.BlockSpec((None, nh, t_new, dh), lambda i: (i, 0, 0, 0))
    o, l, nc = pl.pallas_call(
        functools.partial(_cache_attn_kernel, window // dilation, dilation),
        grid=(b,),
        in_specs=[win, per_q, pl.BlockSpec((None, 2, nh, dh, t_new), lambda i: (i, 0, 0, 0, 0))],
        out_specs=[per_q, per_q, win],
        out_shape=[SDS((b, nh, t_new, dh), F32), SDS((b, nh, t_new, dh), F32), SDS((b, 2, nh, dh, lw), F32)],
        compiler_params=_cp("parallel"),
        name=f"cache_attn_d{dilation}",
    )(ct, qh, kvt)
    tok = lambda a: jnp.transpose(a, (0, 2, 1, 3)).reshape(b * t_new, nh * dh)
    return tok(o), tok(l), jnp.transpose(nc, (0, 4, 1, 2, 3))


def _log_sigmoid(x):
    return jnp.minimum(x, 0.0) - jnp.log1p(jnp.exp(-jnp.abs(x)))


def _mlstm_kernel(n_valid, nsub, *refs):
    for u in range(nsub):
        _mlstm_chunk(n_valid, nsub, u, *refs)


def _mlstm_chunk(n_valid, nsub, u, q_ref, k_ref, v_ref, o_ref, g_ref, bif_ref, ng_ref, c0_ref, n0_ref, m0_ref,
                 h_ref, c1_ref, n1_ref, m1_ref, c_sc, n_sc, m_sc):
    ci = pl.program_id(1)
    nh, dk, dv = MLSTM_HEADS, MLSTM_DK, MLSTM_DV
    L = MLSTM_CHUNK

    @pl.when((ci == 0) & (u == 0))
    def _():
        c_sc[...] = jnp.zeros(c_sc.shape, F32)
        n_sc[...] = jnp.zeros(n_sc.shape, F32)
        m_sc[...] = jnp.zeros(m_sc.shape, F32)
        c_sc[:, 0:dk, :] = c0_ref[...]
        n_sc[:, :, 0:dk] = n0_ref[...]
        m_sc[...] = m0_ref[...]

    def rows(ref):
        x = ref[u * n_valid:(u + 1) * n_valid, :].astype(F32)
        if n_valid < L:
            x = jnp.concatenate([x, jnp.zeros((L - n_valid, x.shape[1]), F32)], axis=0)
        return x

    q = rows(q_ref).astype(BF16)
    k = rows(k_ref)
    v = rows(v_ref).astype(BF16)
    og = rows(o_ref)
    graw = rows(g_ref) + bif_ref[...]
    lane = lax.broadcasted_iota(I32, (L, LANES), 1)
    row = lax.broadcasted_iota(I32, (L, LANES), 0)
    is_i = lane < nh
    is_f = (lane >= nh) & (lane < 2 * nh)
    live = row < n_valid
    gi = jnp.where(is_i, jnp.where(live, graw, NEG_INF), 0.0)
    gf = jnp.where(is_f & live, _log_sigmoid(graw), 0.0)
    tril = (lax.broadcasted_iota(I32, (L, L), 0) >= lax.broadcasted_iota(I32, (L, L), 1))
    bcol = jnp.dot(tril.astype(F32), gf, precision=HIGHEST, preferred_element_type=F32)
    brow = bcol.T
    irow = gi.T
    scale = dk ** -0.5

    def heads(x):
        return jnp.stack([x[:, h * LANES:(h + 1) * LANES] for h in range(nh)], axis=0)

    def bdot(a, b, ca, cb):
        return lax.dot_general(a, b, (((ca,), (cb,)), ((0,), (0,))), preferred_element_type=F32)

    q3, v3 = heads(q), heads(v)
    k3 = heads(k)
    k3b = k3.astype(BF16)
    b_c = jnp.stack([bcol[:, nh + h:nh + h + 1] for h in range(nh)], axis=0)
    b_r = jnp.stack([brow[nh + h:nh + h + 1, :] for h in range(nh)], axis=0)
    i_r = jnp.stack([irow[h:h + 1, :] for h in range(nh)], axis=0)
    m_prev = m_sc[...]
    c3 = c_sc[...]
    n3 = n_sc[...]
    dlog = jnp.where(tril[None], b_c - b_r + i_r, NEG_INF)
    m_inter = b_c + m_prev
    m_t = jnp.maximum(m_inter, jnp.max(dlog, axis=-1, keepdims=True))
    sc = bdot(q3, k3b, 2, 2) * scale * jnp.exp(dlog - m_t)
    inter = jnp.exp(m_inter - m_t)
    num = bdot(sc.astype(BF16), v3, 2, 1) + inter * (bdot(q3, c3.astype(BF16), 2, 1) * scale)
    qn = jnp.sum(q3.astype(F32) * n3, axis=-1, keepdims=True) * scale
    den = jnp.sum(sc, axis=-1, keepdims=True) + inter * qn
    hh = num / jnp.maximum(jnp.abs(den), jnp.exp(-m_t))
    b_last = b_c[:, L - 1:L, :]
    g_r = b_last - b_r + i_r
    m_new = jnp.maximum(b_last + m_prev, jnp.max(g_r, axis=-1, keepdims=True))
    ws = jnp.exp(g_r - m_new)
    decay = jnp.exp(b_last + m_prev - m_new)
    kw = (jnp.stack([k3[h].T for h in range(nh)], axis=0) * ws).astype(BF16)
    c_sc[...] = decay * c3 + bdot(kw, v3, 2, 1)
    ws8 = jnp.broadcast_to(ws, (nh, 8, L)).astype(BF16)
    n_sc[...] = decay * n3 + bdot(ws8, k3b, 2, 1)[:, 0:1, :]
    m_sc[...] = m_new
    mu = jnp.mean(hh, axis=-1, keepdims=True)
    var = jnp.mean(jnp.square(hh - mu), axis=-1, keepdims=True)
    out = jax.nn.sigmoid(heads(og)) * ((hh - mu) * lax.rsqrt(var + EPS) * heads(ng_ref[...]))
    for h in range(nh):
        h_ref[u * n_valid:(u + 1) * n_valid, h * LANES:(h + 1) * LANES] = out[h, 0:n_valid, :].astype(h_ref.dtype)

    @pl.when((ci == pl.num_programs(1) - 1) & (u == nsub - 1))
    def _():
        c1_ref[...] = c_sc[:, 0:dk, :]
        n1_ref[...] = n_sc[:, :, 0:dk]
        m1_ref[...] = m_sc[...]


def _mlstm(qm, km, vm, om, gates, bif_pad, norm_g, c0, n0, m0, n_valid, out_dtype, name):
    b, s, w = qm.shape
    nh, dk, dv = MLSTM_HEADS, MLSTM_DK, MLSTM_DV
    nc = s // n_valid
    assert s % n_valid == 0 and (n_valid == MLSTM_CHUNK or nc == 1)
    nsub = MLSTM_STEP_CHUNKS if nc % MLSTM_STEP_CHUNKS == 0 else 1
    nc //= nsub
    tok = lambda width: pl.BlockSpec((None, nsub * n_valid, width), lambda i, c: (i, c, 0))
    const = lambda shape: pl.BlockSpec(shape, lambda i, c: (0,) * len(shape))
    h, c1, n1, m1 = pl.pallas_call(
        functools.partial(_mlstm_kernel, n_valid, nsub),
        grid=(b, nc),
        in_specs=[tok(w), tok(w), tok(w), tok(w), tok(LANES), const((1, LANES)), const((1, w)),
                  pl.BlockSpec((None, nh, dk, dv), lambda i, c: (i, 0, 0, 0)),
                  pl.BlockSpec((None, nh, 1, dk), lambda i, c: (i, 0, 0, 0)),
                  pl.BlockSpec((None, nh, 1, 1), lambda i, c: (i, 0, 0, 0))],
        out_specs=[tok(w),
                   pl.BlockSpec((None, nh, dk, dv), lambda i, c: (i, 0, 0, 0)),
                   pl.BlockSpec((None, nh, 1, dk), lambda i, c: (i, 0, 0, 0)),
                   pl.BlockSpec((None, nh, 1, 1), lambda i, c: (i, 0, 0, 0))],
        out_shape=[SDS((b, s, w), out_dtype), SDS((b, nh, dk, dv), F32), SDS((b, nh, 1, dk), F32),
                   SDS((b, nh, 1, 1), F32)],
        scratch_shapes=[pltpu.VMEM((nh, DK_PAD, dv), F32), pltpu.VMEM((nh, 1, DK_PAD), F32),
                        pltpu.VMEM((nh, 1, 1), F32)],
        compiler_params=_cp("parallel", "arbitrary"),
        name=name,
    )(qm, km, vm, om, gates, bif_pad, norm_g, c0, n0.reshape(b, nh, 1, dk), m0.reshape(b, nh, 1, 1))
    return h, c1, n1.reshape(b, nh, dk), m1.reshape(b, nh)


def _pack_bf16_pairs(x):
    w = x.shape[1] // 2
    bits = lax.bitcast_convert_type(x.astype(BF16).astype(F32), U32)
    return (bits[:, :w] >> 16) | (bits[:, w:] & jnp.uint32(0xFFFF0000))


def _unpack_pairs_f32(p):
    return lax.bitcast_convert_type(p << 16, F32), lax.bitcast_convert_type(p & jnp.uint32(0xFFFF0000), F32)


def _unpack_bf16_pairs(p):
    lo, hi = _unpack_pairs_f32(p)
    return lo.astype(BF16), hi.astype(BF16)


def _merge_kernel(x_ref, sc1_ref, sh1_ref, g1_ref, sc2_ref, sh2_ref,
                  o0_ref, o1_ref, o2_ref, l0_ref, l1_ref, l2_ref, mo_ref,
                  wg_ref, wpa_ref, wpm_ref, wout_ref, lng_ref, lnb_ref, x1_ref, h2p_ref, tok_sc):
    x = x_ref[...]
    d = x.shape[1]
    h = (x * (1.0 + sc1_ref[0]) + sh1_ref[0]).astype(BF16)
    g = jnp.dot(h, wg_ref[...], preferred_element_type=F32)

    def tokens(ref):
        dil, rows, _ = ref.shape
        if dil == 1:
            return ref[0]
        c = tok_sc.shape[0]
        for r in range(dil):
            for j in range(c):
                tok_sc[j, pl.ds(r, rows, stride=dil), :] = ref[r, :, j * LANES:(j + 1) * LANES]
        return jnp.concatenate([tok_sc[j] for j in range(c)], axis=1)

    l0, l1, l2 = tokens(l0_ref), tokens(l1_ref), tokens(l2_ref)
    lm = jnp.maximum(jnp.maximum(l0, l1), l2)
    e0, e1, e2 = jnp.exp(l0 - lm), jnp.exp(l1 - lm), jnp.exp(l2 - lm)
    ao = (e0 * tokens(o0_ref) + e1 * tokens(o1_ref) + e2 * tokens(o2_ref)) / (e0 + e1 + e2)
    pa = jnp.dot(ao.astype(BF16), wpa_ref[...], preferred_element_type=F32)
    pm = jnp.dot(mo_ref[...].astype(BF16), wpm_ref[...], preferred_element_type=F32)
    merged = jax.nn.sigmoid(g[:, :d]) * pa + jax.nn.sigmoid(g[:, d:]) * pm
    y = jnp.dot(merged.astype(BF16), wout_ref[...], preferred_element_type=F32)
    x1 = _layer_norm(ALPHA * x + g1_ref[0] * y, lng_ref[...], lnb_ref[...])
    x1_ref[...] = x1
    h2p_ref[...] = _pack_bf16_pairs(x1 * (1.0 + sc2_ref[0]) + sh2_ref[0])


def _merge(x, mods, attn, mo, wg, wpa, wpm, wout, lng, lnb, tm, tiles_per_batch, name):
    n, d = x.shape
    tok = lambda width: pl.BlockSpec((tm, width), lambda i: (i, 0))
    const = lambda a: pl.BlockSpec(a.shape, lambda i: (0,) * a.ndim)

    def lattice(a):
        dil, lat = a.shape[1], a.shape[2]
        per_batch = dil * lat // tm
        return pl.BlockSpec((None, dil, tm // dil, a.shape[3]), lambda i: (i // per_batch, 0, i % per_batch, 0))

    return pl.pallas_call(
        _merge_kernel,
        grid=(n // tm,),
        in_specs=[tok(d)] + [_mod_spec(m, tm, tiles_per_batch) for m in mods]
        + [lattice(a) for a in attn] + [tok(mo.shape[1])]
        + [const(a) for a in (wg, wpa, wpm, wout, lng, lnb)],
        out_specs=[tok(d), tok(d // 2)],
        out_shape=[SDS((n, d), F32), SDS((n, d // 2), U32)],
        scratch_shapes=[pltpu.VMEM((GROUP_WIDTH // LANES, tm, LANES), F32)],
        compiler_params=_cp("parallel"),
        name=name,
    )(x, *mods, *attn, mo, wg, wpa, wpm, wout, lng, lnb)


def _first_index_of_max(x, idx, big):
    mx = jnp.max(x, axis=0, keepdims=True)
    first = jnp.min(jnp.where(x == mx, idx, big), axis=0, keepdims=True)
    return mx, first


def _route_kernel(hp_ref, wlo_ref, whi_ref, bias_ref, idx_ref, w_ref, rank_ref, cnt_ref):
    tm = hp_ref.shape[0]
    ne = N_EXPERTS
    per = ne // N_EXPERT_GROUPS
    lo, hi = _unpack_bf16_pairs(hp_ref[...])
    logits = _nt_dot(wlo_ref[...], lo) + _nt_dot(whi_ref[...], hi)
    scores = jax.nn.sigmoid(logits)
    sel = scores + bias_ref[...]
    erow = lax.broadcasted_iota(I32, (ne, tm), 0).astype(F32)
    prow = lax.broadcasted_iota(I32, (per, tm), 0).astype(F32)
    gs = []
    for g in range(N_EXPERT_GROUPS):
        xg = sel[g * per:(g + 1) * per, :]
        m1, i1 = _first_index_of_max(xg, prow, per)
        m2 = jnp.max(jnp.where(prow == i1, NEG_INF, xg), axis=0, keepdims=True)
        gs.append(m1 + m2)
    gs = jnp.concatenate(gs, axis=0)
    grow = lax.broadcasted_iota(I32, gs.shape, 0).astype(F32)
    gkeep = jnp.zeros(gs.shape, F32)
    for _ in range(TOPK_GROUPS):
        _, gi = _first_index_of_max(gs, grow, N_EXPERT_GROUPS)
        pick = grow == gi
        gkeep = jnp.where(pick, 1.0, gkeep)
        gs = jnp.where(pick, NEG_INF, gs)
    keep = jnp.concatenate([jnp.broadcast_to(gkeep[g:g + 1, :], (per, tm)) for g in range(N_EXPERT_GROUPS)], axis=0)
    cand = jnp.where(keep > 0, sel, NEG_INF)
    member = jnp.zeros((ne, tm), F32)
    picks, idxs, ws = [], [], []
    for _ in range(TOP_K):
        _, ei = _first_index_of_max(cand, erow, ne)
        pick = erow == ei
        picks.append(pick)
        idxs.append(ei)
        ws.append(jnp.sum(jnp.where(pick, scores, 0.0), axis=0, keepdims=True))
        cand = jnp.where(pick, NEG_INF, cand)
        member = jnp.where(pick, 1.0, member)
    wsum = ws[0]
    for wk in ws[1:]:
        wsum = wsum + wk
    idx_ref[...] = jnp.concatenate(idxs, axis=0).astype(I32)
    w_ref[...] = jnp.concatenate(ws, axis=0) / wsum * ROUTED_SCALE
    ti = lax.broadcasted_iota(I32, (tm, tm), 0)
    tj = lax.broadcasted_iota(I32, (tm, tm), 1)
    before = (ti < tj).astype(BF16)
    mb = member.astype(BF16)
    prefix = jnp.dot(mb, before, preferred_element_type=F32)
    rank_ref[...] = jnp.concatenate(
        [jnp.sum(jnp.where(p, prefix, 0.0), axis=0, keepdims=True) for p in picks], axis=0).astype(I32)
    cnt_ref[...] = _nt_dot(jnp.ones((8, tm), BF16), mb).astype(I32)


def _route(h2p, w_router_t, bias_col):
    n, half = h2p.shape
    tm = ROUTE_TILE
    nt = n // tm
    wlo = w_router_t[:, :half]
    whi = w_router_t[:, half:]
    pair = lambda dt: SDS((TOP_K, n), dt)
    return pl.pallas_call(
        _route_kernel,
        grid=(nt,),
        in_specs=[pl.BlockSpec((tm, half), lambda i: (i, 0)),
                  pl.BlockSpec(wlo.shape, lambda i: (0, 0)),
                  pl.BlockSpec(whi.shape, lambda i: (0, 0)),
                  pl.BlockSpec(bias_col.shape, lambda i: (0, 0))],
        out_specs=[pl.BlockSpec((TOP_K, tm), lambda i: (0, i))] * 3
        + [pl.BlockSpec((None, 8, N_EXPERTS), lambda i: (i, 0, 0))],
        out_shape=[pair(I32), pair(F32), pair(I32), SDS((nt, 8, N_EXPERTS), I32)],
        compiler_params=_cp("parallel"),
        name="route",
    )(h2p, wlo, whi, bias_col)


def _dest_kernel(idx_ref, rank_ref, base_ref, dest_ref):
    tm = idx_ref.shape[1]
    erow = lax.broadcasted_iota(I32, (N_EXPERTS, tm), 0)
    base = base_ref[...]
    idx = idx_ref[...]
    rows = [jnp.sum(jnp.where(erow == idx[k:k + 1, :], base, 0.0), axis=0, keepdims=True) for k in range(TOP_K)]
    dest = jnp.concatenate(rows, axis=0).astype(I32) + rank_ref[...]
    tile = dest_ref.shape[2]
    for j in range(dest_ref.shape[0]):
        dest_ref[j] = dest[:, j * tile:(j + 1) * tile]


def _dest(idx, rank, base_cols, tile):
    n = idx.shape[1]
    per_route = ROUTE_TILE // tile
    return pl.pallas_call(
        _dest_kernel,
        grid=(n // ROUTE_TILE,),
        in_specs=[pl.BlockSpec((TOP_K, ROUTE_TILE), lambda i: (0, i)),
                  pl.BlockSpec((TOP_K, ROUTE_TILE), lambda i: (0, i)),
                  pl.BlockSpec((None, N_EXPERTS, 1), lambda i: (i, 0, 0))],
        out_specs=pl.BlockSpec((per_route, TOP_K, tile), lambda i: (i, 0, 0)),
        out_shape=SDS((n // tile, TOP_K, tile), I32),
        compiler_params=_cp("parallel"),
        name="dest",
    )(idx, rank, base_cols)


def _row_copy_loop(tile, make_copy, start):
    def body(t, carry):
        for k in range(TOP_K):
            cp = make_copy(k, t)
            if start:
                cp.start(priority=k % 2)
            else:
                cp.wait()
        return carry
    lax.fori_loop(0, tile, body, 0, unroll=8)


def _dest_fetch(dest_hbm, tile_idx, dest_smem, sem_idx):
    n_idx = dest_smem.shape[0]
    start = tile_idx * n_idx
    if not isinstance(start, int):
        start = pl.multiple_of(start, n_idx)
    return pltpu.make_async_copy(dest_hbm.at[pl.ds(start, n_idx)], dest_smem, sem_idx)


def _rows_to_pieces(x, ref):
    c = x.shape[1] // LANES
    for j in range(c):
        ref[pl.ds(j, x.shape[0], stride=c), :] = x[:, j * LANES:(j + 1) * LANES]


def _pieces_to_rows(ref, rows, c):
    return jnp.concatenate([ref[pl.ds(j, rows, stride=c), :] for j in range(c)], axis=1)


def _piece(ref, row, c):
    return ref.at[pl.ds(pl.multiple_of(row * c, c), c)]


def _scatter_kernel(pends_ref, dest_hbm, hp_ref, xs_out, stage, zeros_sc, dest_smem, sem_idx, sem_rows, sem_zero):
    i = pl.program_id(0)
    n = pl.num_programs(0)
    tile = hp_ref.shape[0]
    c = hp_ref.shape[1] // LANES
    slot = lax.rem(i, 2)
    idx_cp = _dest_fetch(dest_hbm, i, dest_smem, sem_idx)
    idx_cp.start()

    @pl.when(i == 0)
    def _():
        blk = zeros_sc.shape[0]
        zeros_sc[...] = jnp.zeros(zeros_sc.shape, zeros_sc.dtype)

        def zero_copy(e):
            start = jnp.maximum(pends_ref[e] * c - blk, 0)
            return pltpu.make_async_copy(zeros_sc, xs_out.at[pl.ds(pl.multiple_of(start, blk), blk)], sem_zero)

        def start_body(e, carry):
            zero_copy(e).start()
            return carry

        def wait_body(e, carry):
            zero_copy(e).wait()
            return carry

        lax.fori_loop(0, N_EXPERTS, start_body, 0)
        lax.fori_loop(0, N_EXPERTS, wait_body, 0)

    def row_copy(k, t):
        return pltpu.make_async_copy(_piece(stage.at[slot], t, c), _piece(xs_out, dest_smem[k * tile + t], c),
                                     sem_rows.at[slot])

    def wait_tile(s):
        _row_copy_loop(tile, lambda k, t: pltpu.make_async_copy(
            stage.at[s, pl.ds(0, c)], xs_out.at[pl.ds(0, c)], sem_rows.at[s]), False)

    @pl.when(i >= 2)
    def _():
        wait_tile(slot)

    _rows_to_pieces(hp_ref[...], stage.at[slot])
    idx_cp.wait()
    _row_copy_loop(tile, row_copy, True)

    @pl.when(i == n - 1)
    def _():
        @pl.when(n >= 2)
        def _():
            wait_tile(1 - slot)
        wait_tile(slot)


def _scatter_rows(pends, dest_flat, h2p, n_rows):
    n, half = h2p.shape
    tile = MOE_TILE
    c = half // LANES
    grid_spec = pltpu.PrefetchScalarGridSpec(
        num_scalar_prefetch=1,
        grid=(n // tile,),
        in_specs=[pl.BlockSpec(memory_space=pl.ANY),
                  pl.BlockSpec((tile, half), lambda i, pe: (i, 0))],
        out_specs=pl.BlockSpec(memory_space=pl.ANY),
        scratch_shapes=[pltpu.VMEM((2, tile * c, LANES), U32), pltpu.VMEM((MOE_BLOCK * c, LANES), U32),
                        pltpu.SMEM((TOP_K * tile,), I32),
                        pltpu.SemaphoreType.DMA, pltpu.SemaphoreType.DMA((2,)), pltpu.SemaphoreType.DMA],
    )
    return pl.pallas_call(
        _scatter_kernel,
        grid_spec=grid_spec,
        out_shape=SDS((n_rows * c, LANES), U32),
        compiler_params=_cp("arbitrary"),
        name="scatter_rows",
    )(pends, dest_flat, h2p)


def _expert_kernel(blk_exp_ref, n_used_ref, xs_ref, win_ref, wout_ref, ys_ref, win_sc, wout_sc):
    i = pl.program_id(0)
    prev = blk_exp_ref[jnp.maximum(i - 1, 0)]

    @pl.when((i == 0) | (blk_exp_ref[i] != prev))
    def _():
        win_sc[...] = win_ref[...].astype(BF16)
        wout_sc[...] = wout_ref[...].astype(BF16)

    @pl.when(i < n_used_ref[0])
    def _():
        half = win_sc.shape[0] // 2
        lo, hi = _unpack_bf16_pairs(_pieces_to_rows(xs_ref, MOE_BLOCK, half // LANES))
        ag = (jnp.dot(lo, win_sc[0:half, :], preferred_element_type=F32)
              + jnp.dot(hi, win_sc[half:2 * half, :], preferred_element_type=F32))
        a = ag[:, :D_EXPERT]
        g = ag[:, D_EXPERT:]
        mid = (a * jax.nn.sigmoid(a) * g).astype(BF16)
        _rows_to_pieces(_pack_bf16_pairs(jnp.dot(mid, wout_sc[...], preferred_element_type=F32)), ys_ref)

    @pl.when(i >= n_used_ref[0])
    def _():
        ys_ref[...] = jnp.zeros(ys_ref.shape, ys_ref.dtype)


def _experts(xs, blk_exp, n_used, w_exp_in, w_exp_out):
    d = w_exp_in.shape[1]
    cx = cy = d // 2 // LANES
    n_rows = xs.shape[0] // cx
    nb = n_rows // MOE_BLOCK
    grid_spec = pltpu.PrefetchScalarGridSpec(
        num_scalar_prefetch=2,
        grid=(nb,),
        in_specs=[pl.BlockSpec((MOE_BLOCK * cx, LANES), lambda i, be, nu: (jnp.where(i < nu[0], i, 0), 0)),
                  pl.BlockSpec((None, d, 2 * D_EXPERT), lambda i, be, nu: (be[i], 0, 0)),
                  pl.BlockSpec((None, D_EXPERT, d), lambda i, be, nu: (be[i], 0, 0))],
        out_specs=pl.BlockSpec((MOE_BLOCK * cy, LANES), lambda i, be, nu: (i, 0)),
        scratch_shapes=[pltpu.VMEM((d, 2 * D_EXPERT), BF16), pltpu.VMEM((D_EXPERT, d), BF16)],
    )
    return pl.pallas_call(
        _expert_kernel,
        grid_spec=grid_spec,
        out_shape=SDS((n_rows * cy, LANES), U32),
        compiler_params=_cp("arbitrary"),
        name="experts",
    )(blk_exp, n_used, xs, w_exp_in, w_exp_out)


def _final_kernel(dest_hbm, x1_ref, hp_ref, w_ref, g2_ref, ys_hbm, wsin_ref, wsout_ref, lng_ref, lnb_ref,
                  y_ref, rows_sc, dest_smem, sem_idx, sem_rows, *, tile_offset):
    i = pl.program_id(0)
    n = pl.num_programs(0)
    tile = x1_ref.shape[0]
    c = hp_ref.shape[1] // LANES
    slot = lax.rem(i, 2)

    def start_rows(s):
        _row_copy_loop(tile, lambda k, t: pltpu.make_async_copy(
            _piece(ys_hbm, dest_smem[k * tile + t], c), _piece(rows_sc.at[s, k], t, c), sem_rows.at[s]), True)

    @pl.when(i == 0)
    def _():
        first = _dest_fetch(dest_hbm, tile_offset, dest_smem, sem_idx)
        first.start()
        first.wait()
        start_rows(0)

    nxt = _dest_fetch(dest_hbm, jnp.minimum(i + 1, n - 1) + tile_offset, dest_smem, sem_idx)

    @pl.when(i + 1 < n)
    def _():
        nxt.start()

    half = hp_ref.shape[1]
    lo, hi = _unpack_bf16_pairs(hp_ref[...])
    ag = (jnp.dot(lo, wsin_ref[0:half, :], preferred_element_type=F32)
          + jnp.dot(hi, wsin_ref[half:2 * half, :], preferred_element_type=F32))
    a = ag[:, :D_SHARED]
    g = ag[:, D_SHARED:]
    y2 = jnp.dot((a * jax.nn.sigmoid(a) * g).astype(BF16), wsout_ref[...], preferred_element_type=F32)
    w_cols = jnp.concatenate([w_ref[...], jnp.zeros((tile - TOP_K, tile), F32)], axis=0).T

    @pl.when(i + 1 < n)
    def _():
        nxt.wait()
        start_rows(1 - slot)

    _row_copy_loop(tile, lambda k, t: pltpu.make_async_copy(
        ys_hbm.at[pl.ds(0, c)], rows_sc.at[slot, 0, pl.ds(0, c)], sem_rows.at[slot]), False)
    y_lo, y_hi = y2[:, :half], y2[:, half:]
    for k in range(TOP_K):
        r_lo, r_hi = _unpack_pairs_f32(_pieces_to_rows(rows_sc.at[slot, k], tile, c))
        y_lo = y_lo + w_cols[:, k:k + 1] * r_lo
        y_hi = y_hi + w_cols[:, k:k + 1] * r_hi
    y2 = jnp.concatenate([y_lo, y_hi], axis=1)
    y_ref[...] = _layer_norm(ALPHA * x1_ref[...] + g2_ref[0] * y2, lng_ref[...], lnb_ref[...])


def _final(dest_flat, x1, h2p, w_top, g2_3, ys, wsin, wsout, lng, lnb, tile_offset, tiles_per_batch, name):
    n, d = x1.shape
    tile = MOE_TILE
    assert tile == LANES
    const = lambda a: pl.BlockSpec(a.shape, lambda i: (0,) * a.ndim)
    return pl.pallas_call(
        functools.partial(_final_kernel, tile_offset=tile_offset),
        grid=(n // tile,),
        in_specs=[pl.BlockSpec(memory_space=pl.ANY),
                  pl.BlockSpec((tile, d), lambda i: (i, 0)),
                  pl.BlockSpec((tile, d // 2), lambda i: (i, 0)),
                  pl.BlockSpec((TOP_K, tile), lambda i: (0, i)),
                  _mod_spec(g2_3, tile, tiles_per_batch),
                  pl.BlockSpec(memory_space=pl.ANY),
                  const(wsin), const(wsout), const(lng), const(lnb)],
        out_specs=pl.BlockSpec((tile, d), lambda i: (i, 0)),
        out_shape=SDS((n, d), F32),
        scratch_shapes=[pltpu.VMEM((2, TOP_K, tile * (d // 2 // LANES), LANES), U32), pltpu.SMEM((TOP_K * tile,), I32),
                        pltpu.SemaphoreType.DMA, pltpu.SemaphoreType.DMA((2,))],
        compiler_params=_cp("arbitrary"),
        name=name,
    )(dest_flat, x1, h2p, w_top, g2_3, ys, wsin, wsout, lng, lnb)


def _sorted_layout(counts):
    totals = jnp.sum(counts, axis=0)
    padded = (totals + MOE_BLOCK - 1) // MOE_BLOCK * MOE_BLOCK
    pends = jnp.cumsum(padded)
    base = (pends - padded)[None, :] + jnp.cumsum(counts, axis=0) - counts
    return base.astype(I32), pends


def _moe_routed(h2p, w_router, router_bias, w_exp_in, w_exp_out):
    n = h2p.shape[0]
    idx, w_top, rank, cnt = _route(h2p, w_router.T.astype(BF16), router_bias.reshape(N_EXPERTS, 1))
    base, pends = _sorted_layout(cnt[:, 0, :])
    n_blocks = -(-(n * TOP_K + N_EXPERTS * (MOE_BLOCK - 1)) // MOE_BLOCK)
    blk_start = jnp.arange(n_blocks, dtype=I32) * MOE_BLOCK
    blk_exp = jnp.minimum(jnp.sum((pends[None, :] <= blk_start[:, None]).astype(I32), axis=1), N_EXPERTS - 1)
    n_used = (pends[-1:] // MOE_BLOCK).astype(I32)
    dest = _dest(idx, rank, base.astype(F32)[:, :, None], MOE_TILE).reshape(-1)
    xs = _scatter_rows(pends.astype(I32), dest, h2p, n_blocks * MOE_BLOCK)
    ys = _experts(xs, blk_exp, n_used, w_exp_in, w_exp_out)
    return dest, w_top, ys


def _pad_heads(w, nh, dk):
    d = w.shape[0]
    return jnp.pad(w.reshape(d, nh, dk), ((0, 0), (0, 0), (0, DK_PAD - dk))).reshape(d, nh * DK_PAD)


def _split_w_in(w_in):
    aw = N_GROUPS * GROUP_WIDTH
    nh, dk, dv = MLSTM_HEADS, MLSTM_DK, MLSTM_DV
    offs = np.cumsum([0, aw, aw, aw, nh * dk, nh * dk, nh * dv, nh * dv, nh, nh, w_in.shape[0], w_in.shape[0]])
    seg = [w_in[:, offs[i]:offs[i + 1]] for i in range(11)]
    q_a, k_a, v_a, q_m, k_m, v_m, o_m, i_m, f_m, g_a, g_b = seg
    bf = lambda a: a.astype(BF16)
    w_gates = jnp.pad(jnp.concatenate([i_m, f_m], axis=1), ((0, 0), (0, LANES - 2 * nh)))
    w_gates_hi = w_gates.astype(BF16)
    w_gates = jnp.concatenate([w_gates_hi, (w_gates - w_gates_hi.astype(F32)).astype(BF16)], axis=1)
    return dict(qa=bf(q_a), kva=bf(jnp.concatenate([k_a, v_a], axis=1)),
                qm=bf(_pad_heads(q_m, nh, dk)), km=bf(_pad_heads(k_m, nh, dk)), vm=bf(v_m), om=bf(o_m),
                gates=w_gates, g=bf(jnp.concatenate([g_a, g_b], axis=1)))


def _mixing(x, mods, tm, tiles_per_batch, batch, seq, caches, states, wts, prompt):
    n, d = x.shape
    sc1, sh1, g1 = mods["scale1"], mods["shift1"], mods["gate1"]
    sc2, sh2 = mods["scale2"], mods["shift2"]
    act = BF16 if prompt else F32
    if prompt:
        tail = min(max(w for w, _ in ATTN_GROUPS), seq)
        qkv_l, kv_tail = _proj_lattice(x, sc1, sh1, wts["qa"], wts["kva"], batch, seq, tm, tail)
    else:
        q_a, kv_a = _proj(x, sc1, sh1, [wts["qa"], wts["kva"]], [F32, F32], None, tm, tiles_per_batch, "proj_attn_s")
    qm, km, vm, om, gates = _proj(x, sc1, sh1, [wts["qm"], wts["km"], wts["vm"], wts["om"]],
                                  [act, act, act, F32], wts["gates"], tm, tiles_per_batch,
                                  "proj_mlstm_p" if prompt else "proj_mlstm_s")
    attn_o, attn_l, bufs = [], [], []
    for g, (window, dilation) in enumerate(ATTN_GROUPS):
        if prompt:
            o, l = _band_attn(*qkv_l[g], window)
            keep = min(window, seq)
            kv3 = kv_tail.reshape(batch, tail, 2, N_GROUPS, GROUP_WIDTH)[:, tail - keep:, :, g, :]
            bufs.append(kv3.reshape(1, batch, keep, 2, ATTN_HEADS, ATTN_HEAD_DIM))
        else:
            o, l, nc = _cache_attn(q_a.reshape(batch, seq, -1), kv_a.reshape(batch, seq, -1), caches[g], g,
                                   window, dilation)
            o, l = o[None, None], l[None, None]
            bufs.append(nc[None])
        attn_o.append(o)
        attn_l.append(l)
    r3 = lambda a: a.reshape(batch, seq, a.shape[-1])
    n_valid = MLSTM_CHUNK if prompt else seq
    mo, c1, n1, m1 = _mlstm(r3(qm), r3(km), r3(vm), r3(om), r3(gates), wts["bif"], wts["norm_g"],
                            states[0], states[1], states[2], n_valid, act,
                            "mlstm_p" if prompt else "mlstm_s")
    x1, h2p = _merge(x, [sc1, sh1, g1, sc2, sh2], attn_o + attn_l, mo.reshape(n, -1),
                     wts["g"], wts["pa"], wts["pm"], wts["out"], wts["ln1_g"], wts["ln1_b"],
                     tm, tiles_per_batch, "merge_p" if prompt else "merge_s")
    return x1, h2p, bufs, (c1[None], n1[None], m1[None])


def _mod_pieces(mod, d, rows_per_batch, tm):
    names = ("shift1", "scale1", "gate1", "shift2", "scale2", "gate2")
    out = {}
    for p, name in enumerate(names):
        piece = mod[:, p * d:(p + 1) * d]
        if rows_per_batch % tm == 0:
            out[name] = piece[:, None, :]
        else:
            out[name] = jnp.repeat(piece, rows_per_batch, axis=0).reshape(-1, tm, d)
    return out


def kernel(x_prompt, x_sample, cache_kv_w128, cache_kv_w512, cache_kv_w2048, state_mlstm_C, state_mlstm_n, state_mlstm_m, c_prompt, c_sample, w_ada, b_ada, w_in, b_if, mlstm_norm_g, w_proj_attn, w_proj_mlstm, w_out, ln1_g, ln1_b, w_router, router_bias, w_exp_in, w_exp_out, w_sh_in, w_sh_out, ln2_g, ln2_b):
    assert w_ada.shape[0] == DEPTH
    bp, sp, d = x_prompt.shape
    bs, ss, _ = x_sample.shape
    nh = MLSTM_HEADS
    np_, ns = bp * sp, bs * ss

    mod = _ada(jnp.concatenate([c_prompt, c_sample], axis=0), w_ada[0], b_ada[0])
    wts = _split_w_in(w_in[0])
    wts.update(
        bif=jnp.pad(b_if[0], (0, LANES - 2 * nh)).reshape(1, LANES),
        norm_g=mlstm_norm_g[0].reshape(1, -1),
        pa=w_proj_attn[0].astype(BF16), pm=w_proj_mlstm[0].astype(BF16), out=w_out[0].astype(BF16),
        ln1_g=ln1_g[0].reshape(1, d), ln1_b=ln1_b[0].reshape(1, d))

    tm_p, tm_s = 512, 256
    mods_p = _mod_pieces(mod[:bp], d, sp, tm_p)
    mods_s = _mod_pieces(mod[bp:], d, ss, tm_s)
    zeros_p = (jnp.zeros((bp, nh, MLSTM_DK, MLSTM_DV), F32), jnp.zeros((bp, nh, MLSTM_DK), F32),
               jnp.zeros((bp, nh), F32))
    x1p, h2p_p, bufs_p, st_p = _mixing(x_prompt.reshape(np_, d), mods_p, tm_p, sp // tm_p, bp, sp, None,
                                       zeros_p, wts, True)
    caches = (cache_kv_w128[0], cache_kv_w512[0], cache_kv_w2048[0])
    states = (state_mlstm_C[0], state_mlstm_n[0], state_mlstm_m[0])
    x1s, h2p_s, bufs_s, st_s = _mixing(x_sample.reshape(ns, d), mods_s, tm_s, 1, bs, ss, caches, states, wts, False)

    h2p = jnp.concatenate([h2p_p, h2p_s], axis=0)
    dest, w_top, ys = _moe_routed(h2p, w_router[0], router_bias[0], w_exp_in[0], w_exp_out[0])
    wsin, wsout = w_sh_in[0].astype(BF16), w_sh_out[0].astype(BF16)
    lng, lnb = ln2_g[0].reshape(1, d), ln2_b[0].reshape(1, d)
    g2_p = _mod_pieces(mod[:bp], d, sp, MOE_TILE)["gate2"]
    g2_s = _mod_pieces(mod[bp:], d, ss, MOE_TILE)["gate2"]
    y_p = _final(dest, x1p, h2p_p, w_top[:, :np_], g2_p, ys, wsin, wsout, lng, lnb, 0, sp // MOE_TILE, "final_p")
    y_s = _final(dest, x1s, h2p_s, w_top[:, np_:], g2_s, ys, wsin, wsout, lng, lnb, np_ // MOE_TILE, 1, "final_s")

    return (y_p.reshape(bp, sp, d), y_s.reshape(bs, ss, d),
            bufs_p[0], bufs_p[1], bufs_p[2], st_p[0], st_p[1], st_p[2],
            bufs_s[0], bufs_s[1], bufs_s[2], st_s[0], st_s[1], st_s[2])
```
